```python
import math
import jax, jax.numpy as jnp
from jax import lax
import numpy as np

D_MODEL = 1024
BATCH = 8
SEQ = 4096
DEPTH = 4
DEC_BATCH = 16
DEC_SEQ = 32
PAST_LEN = 2048

CHUNK = 64
GDN_HEAD_DIM = 128
GDN_WIDTH = D_MODEL // 2
GDN_HEADS = GDN_WIDTH // GDN_HEAD_DIM
GDN_CONV = 4
QKV_COLS = 3 * GDN_WIDTH
S5_WIDTH = D_MODEL // 4
S5_GROUP = 16
S5_GROUPS = S5_WIDTH // S5_GROUP
S5_STATE = 64
CC_WIDTH = D_MODEL - GDN_WIDTH - S5_WIDTH
CC_KERNEL = 31
FFN_HIDDEN = -(-8 * D_MODEL // (3 * 256)) * 256
PLE_DIM = 256
OFF_Z = QKV_COLS
OFF_BA = OFF_Z + GDN_WIDTH
OFF_S5 = OFF_BA + 2 * GDN_HEADS
OFF_CC = OFF_S5 + S5_WIDTH
IN_COLS = OFF_CC + 2 * CC_WIDTH

kernel_name = 'hybrid_streaming_gdn_s5_conformer_step'

F32 = jnp.float32


def rmsnorm(x, g, eps=1e-6):
    xf = x.astype(F32)
    y = xf * lax.rsqrt(jnp.mean(xf * xf, axis=-1, keepdims=True) + eps)
    return (y * g.astype(F32)).astype(x.dtype)


def layernorm(x, g, b, eps=1e-5):
    xf = x.astype(F32)
    mu = jnp.mean(xf, axis=-1, keepdims=True)
    var = jnp.mean(jnp.square(xf - mu), axis=-1, keepdims=True)
    return ((xf - mu) * lax.rsqrt(var + eps) * g.astype(F32) + b.astype(F32)).astype(x.dtype)


def l2norm(x, eps=1e-6):
    return x * lax.rsqrt(jnp.sum(x * x, axis=-1, keepdims=True) + eps)


def causal_dwconv(x, buf, w):
    width, ch = w.shape
    xp = jnp.concatenate([buf.astype(x.dtype), x], axis=1)
    y = lax.conv_general_dilated(xp, w[:, None, :].astype(x.dtype), window_strides=(1,), padding='VALID',
                                 dimension_numbers=('NWC', 'WIO', 'NWC'), feature_group_count=ch)
    return y, xp[:, xp.shape[1] - (width - 1):]


def gated_delta_rule(q, k, v, g, beta, s0):
    bsz, t, h, dk = q.shape
    dv = v.shape[-1]
    c = CHUNK if t % CHUNK == 0 else t
    n = t // c

    def chunks(a):
        return jnp.swapaxes(a.reshape((bsz, n, c) + a.shape[2:]), 2, 3)

    q, k, v, g, beta = chunks(q), chunks(k), chunks(v), chunks(g), chunks(beta)
    gc = jnp.cumsum(g, axis=-1)
    idx = jnp.arange(c)
    causal = idx[:, None] >= idx[None, :]
    strict = idx[:, None] > idx[None, :]
    decay = jnp.exp(jnp.where(causal, gc[..., :, None] - gc[..., None, :], -jnp.inf))
    kb = k * beta[..., None]
    kk = jnp.einsum('bnhck,bnhsk->bnhcs', kb, k) * decay
    a_mat = jnp.eye(c, dtype=F32) + jnp.where(strict, kk, 0.0)
    rhs = jnp.concatenate([v * beta[..., None], kb * jnp.exp(gc)[..., None]], axis=-1)
    sol = lax.linalg.triangular_solve(a_mat, rhs, left_side=True, lower=True, unit_diagonal=True)
    u, w = sol[..., :dv], sol[..., dv:]
    aqk = jnp.einsum('bnhck,bnhsk->bnhcs', q, k) * decay
    g_last = gc[..., -1]
    qg = q * jnp.exp(gc)[..., None]
    kd = k * jnp.exp(g_last[..., None] - gc)[..., None]

    def step(s, xs):
        qg_c, kd_c, u_c, w_c, aqk_c, dl_c = xs
        v_new = u_c - jnp.einsum('bhck,bhkv->bhcv', w_c, s)
        o = jnp.einsum('bhck,bhkv->bhcv', qg_c, s) + jnp.einsum('bhcs,bhsv->bhcv', aqk_c, v_new)
        s = s * dl_c[..., None, None] + jnp.einsum('bhck,bhcv->bhkv', kd_c, v_new)
        return s, o

    xs = tuple(jnp.moveaxis(a, 1, 0) for a in (qg, kd, u, w, aqk, jnp.exp(g_last)))
    s_final, o = lax.scan(step, s0, xs)
    o = jnp.swapaxes(jnp.moveaxis(o, 0, 1), 2, 3).reshape(bsz, t, h, dv)
    return o, s_final


def gdn_mixer(qkv_pre, z, ba, buf, s0, conv_w, a_log, dt_bias, norm_w):
    bsz, t, _ = qkv_pre.shape
    qkv, new_buf = causal_dwconv(qkv_pre, buf, conv_w)
    qkv = jax.nn.silu(qkv.astype(F32))
    q, k, v = jnp.split(qkv, 3, axis=-1)
    q = l2norm(q.reshape(bsz, t, GDN_HEADS, GDN_HEAD_DIM)) * (GDN_HEAD_DIM ** -0.5)
    k = l2norm(k.reshape(bsz, t, GDN_HEADS, GDN_HEAD_DIM))
    v = v.reshape(bsz, t, GDN_HEADS, GDN_HEAD_DIM)
    b_raw, a_raw = jnp.split(ba.astype(F32), 2, axis=-1)
    beta = jax.nn.sigmoid(b_raw)
    g = -jnp.exp(a_log.astype(F32)) * jax.nn.softplus(a_raw + dt_bias.astype(F32))
    o, s_new = gated_delta_rule(q, k, v, g, beta, s0.astype(F32))
    o = o * lax.rsqrt(jnp.mean(o * o, axis=-1, keepdims=True) + 1e-6) * norm_w.astype(F32)
    o = o * jax.nn.silu(z.astype(F32).reshape(bsz, t, GDN_HEADS, GDN_HEAD_DIM))
    return o.reshape(bsz, t, GDN_WIDTH).astype(qkv_pre.dtype), s_new, new_buf


def s5_mixer(u, h0, lam_re, lam_im, log_dt, b_re, b_im, c_re, c_im, d, glu_w, glu_b):
    bsz, t, _ = u.shape
    uf = u.astype(F32).reshape(bsz, t, S5_GROUPS, S5_GROUP)
    lam = lax.complex(lam_re.astype(F32), lam_im.astype(F32))
    dt = jnp.exp(log_dt.astype(F32))[:, None]
    lam_bar = jnp.exp(lam * dt)
    b_bar = ((lam_bar - 1.0) / lam)[..., None] * lax.complex(b_re.astype(F32), b_im.astype(F32))
    cmat = lax.complex(c_re.astype(F32), c_im.astype(F32))
    bu = jnp.einsum('gpc,btgc->btgp', b_bar, uf.astype(jnp.complex64))
    h0c = lax.complex(h0[..., 0].astype(F32), h0[..., 1].astype(F32))
    bu = bu.at[:, 0].add(lam_bar * h0c)
    a = jnp.broadcast_to(lam_bar, bu.shape)

    def combine(e1, e2):
        a1, b1 = e1
        a2, b2 = e2
        return a2 * a1, a2 * b1 + b2

    _, xs = lax.associative_scan(combine, (a, bu), axis=1)
    y = jnp.einsum('gcp,btgp->btgc', cmat, xs).real + d.astype(F32).reshape(S5_GROUPS, S5_GROUP) * uf
    h_last = xs[:, -1]
    h_new = jnp.stack([h_last.real, h_last.imag], axis=-1)
    y = jax.nn.gelu(y.reshape(bsz, t, S5_WIDTH)).astype(u.dtype)
    y = y * jax.nn.sigmoid(y @ glu_w + glu_b)
    return y, h_new


def conformer_conv(c_in, buf, dw_w, dw_b, ln_g, ln_b):
    a, gate = jnp.split(c_in, 2, axis=-1)
    x = a * jax.nn.sigmoid(gate)
    xc, new_buf = causal_dwconv(x, buf, dw_w)
    xc = xc + dw_b
    return jax.nn.silu(layernorm(xc, ln_g, ln_b)), new_buf


def trunk(x, p, st_gdn, st_gconv, st_s5, st_conv, W):
    h = x
    n_gdn, n_gconv, n_s5, n_conv = [], [], [], []
    for i in range(DEPTH):
        hn = rmsnorm(h, W['norm_mix'][i])
        proj = hn @ W['w_in'][i]
        qkv_pre, z, ba, u5, cc = jnp.split(proj, [OFF_Z, OFF_BA, OFF_S5, OFF_CC], axis=-1)
        ya, s_gdn, b_gconv = gdn_mixer(qkv_pre, z, ba, st_gconv[i], st_gdn[i], W['gdn_conv_w'][i],
                                       W['gdn_a_log'][i], W['gdn_dt_bias'][i], W['gdn_norm'][i])
        yb, s_s5 = s5_mixer(u5, st_s5[i], W['s5_lam_re'][i], W['s5_lam_im'][i], W['s5_log_dt'][i],
                            W['s5_b_re'][i], W['s5_b_im'][i], W['s5_c_re'][i], W['s5_c_im'][i],
                            W['s5_d'][i], W['s5_glu_w'][i], W['s5_glu_b'][i])
        yc, b_conv = conformer_conv(cc, st_conv[i], W['cc_dw_w'][i], W['cc_dw_b'][i],
                                    W['cc_ln_g'][i], W['cc_ln_b'][i])
        mix = jnp.concatenate([ya.astype(h.dtype), yb.astype(h.dtype), yc.astype(h.dtype)], axis=-1)
        h = h + mix @ W['w_out'][i]
        hf = rmsnorm(h, W['norm_ffn'][i])
        h = h + (jax.nn.silu(hf @ W['ffn_w1'][i]) * (hf @ W['ffn_w3'][i])) @ W['ffn_w2'][i]
        h = h + (p[i] @ W['pe_w'][i]) * jax.nn.sigmoid(h @ W['pe_gate_w'][i])
        n_gdn.append(s_gdn)
        n_gconv.append(b_gconv)
        n_s5.append(s_s5)
        n_conv.append(b_conv)
    y = rmsnorm(h, W['norm_final'])
    return y, jnp.stack(n_gdn), jnp.stack(n_gconv), jnp.stack(n_s5), jnp.stack(n_conv)


def setup_inputs(seed: int = 0) -> dict:
    key = jax.random.key(seed)
    ks = iter(jax.random.split(key, 48))

    def nrm(shape, s=1.0):
        return s * jax.random.normal(next(ks), shape, F32)

    def unif(shape, lo, hi):
        return jax.random.uniform(next(ks), shape, F32, lo, hi)

    L = DEPTH
    x_prompt = nrm((BATCH, SEQ, D_MODEL))
    x_sample = nrm((DEC_BATCH, DEC_SEQ, D_MODEL))
    p_prompt = nrm((L, BATCH, SEQ, PLE_DIM))
    p_sample = nrm((L, DEC_BATCH, DEC_SEQ, PLE_DIM))
    state_gdn = nrm((L, DEC_BATCH, GDN_HEADS, GDN_HEAD_DIM, GDN_HEAD_DIM), 0.1)
    state_gdn_conv = nrm((L, DEC_BATCH, GDN_CONV - 1, QKV_COLS))
    state_s5 = nrm((L, DEC_BATCH, S5_GROUPS, S5_STATE, 2), 0.5)
    state_conv = nrm((L, DEC_BATCH, CC_KERNEL - 1, CC_WIDTH))
    norm_mix = 1.0 + nrm((L, D_MODEL), 0.02)
    w_in = nrm((L, D_MODEL, IN_COLS), D_MODEL ** -0.5)
    gdn_conv_w = nrm((L, GDN_CONV, QKV_COLS), GDN_CONV ** -0.5)
    gdn_a_log = jnp.log(unif((L, GDN_HEADS), 1.0, 16.0))
    dt = jnp.exp(unif((L, GDN_HEADS), math.log(1e-3), math.log(1e-1)))
    gdn_dt_bias = dt + jnp.log(-jnp.expm1(-dt))
    gdn_norm = 1.0 + nrm((L, GDN_HEAD_DIM), 0.02)
    n_idx = jnp.arange(S5_STATE, dtype=F32)
    s5_lam_re = -0.5 + nrm((L, S5_GROUPS, S5_STATE), 0.01)
    s5_lam_im = jnp.pi * n_idx + nrm((L, S5_GROUPS, S5_STATE), 0.01)
    s5_log_dt = unif((L, S5_GROUPS), math.log(1e-3), math.log(1e-1))
    s5_b_re = nrm((L, S5_GROUPS, S5_STATE, S5_GROUP), (2 * S5_GROUP) ** -0.5)
    s5_b_im = nrm((L, S5_GROUPS, S5_STATE, S5_GROUP), (2 * S5_GROUP) ** -0.5)
    s5_c_re = nrm((L, S5_GROUPS, S5_GROUP, S5_STATE), S5_STATE ** -0.5)
    s5_c_im = nrm((L, S5_GROUPS, S5_GROUP, S5_STATE), S5_STATE ** -0.5)
    s5_d = nrm((L, S5_WIDTH))
    s5_glu_w = nrm((L, S5_WIDTH, S5_WIDTH), S5_WIDTH ** -0.5)
    s5_glu_b = nrm((L, S5_WIDTH), 0.01)
    cc_dw_w = nrm((L, CC_KERNEL, CC_WIDTH), CC_KERNEL ** -0.5)
    cc_dw_b = nrm((L, CC_WIDTH), 0.01)
    cc_ln_g = 1.0 + nrm((L, CC_WIDTH), 0.02)
    cc_ln_b = nrm((L, CC_WIDTH), 0.01)
    w_out = nrm((L, D_MODEL, D_MODEL), D_MODEL ** -0.5)
    norm_ffn = 1.0 + nrm((L, D_MODEL), 0.02)
    ffn_w1 = nrm((L, D_MODEL, FFN_HIDDEN), D_MODEL ** -0.5)
    ffn_w3 = nrm((L, D_MODEL, FFN_HIDDEN), D_MODEL ** -0.5)
    ffn_w2 = nrm((L, FFN_HIDDEN, D_MODEL), FFN_HIDDEN ** -0.5)
    pe_w = nrm((L, PLE_DIM, D_MODEL), PLE_DIM ** -0.5)
    pe_gate_w = nrm((L, D_MODEL, D_MODEL), D_MODEL ** -0.5)
    norm_final = 1.0 + nrm((D_MODEL,), 0.02)
    return {'x_prompt': x_prompt, 'x_sample': x_sample, 'p_prompt': p_prompt, 'p_sample': p_sample,
            'state_gdn': state_gdn, 'state_gdn_conv': state_gdn_conv, 'state_s5': state_s5, 'state_conv': state_conv,
            'norm_mix': norm_mix, 'w_in': w_in, 'gdn_conv_w': gdn_conv_w, 'gdn_a_log': gdn_a_log,
            'gdn_dt_bias': gdn_dt_bias, 'gdn_norm': gdn_norm, 's5_lam_re': s5_lam_re, 's5_lam_im': s5_lam_im,
            's5_log_dt': s5_log_dt, 's5_b_re': s5_b_re, 's5_b_im': s5_b_im, 's5_c_re': s5_c_re, 's5_c_im': s5_c_im,
            's5_d': s5_d, 's5_glu_w': s5_glu_w, 's5_glu_b': s5_glu_b, 'cc_dw_w': cc_dw_w, 'cc_dw_b': cc_dw_b,
            'cc_ln_g': cc_ln_g, 'cc_ln_b': cc_ln_b, 'w_out': w_out, 'norm_ffn': norm_ffn, 'ffn_w1': ffn_w1,
            'ffn_w3': ffn_w3, 'ffn_w2': ffn_w2, 'pe_w': pe_w, 'pe_gate_w': pe_gate_w, 'norm_final': norm_final}


def reference(x_prompt, x_sample, p_prompt, p_sample, state_gdn, state_gdn_conv, state_s5, state_conv,
              norm_mix, w_in, gdn_conv_w, gdn_a_log, gdn_dt_bias, gdn_norm, s5_lam_re, s5_lam_im, s5_log_dt,
              s5_b_re, s5_b_im, s5_c_re, s5_c_im, s5_d, s5_glu_w, s5_glu_b, cc_dw_w, cc_dw_b, cc_ln_g, cc_ln_b,
              w_out, norm_ffn, ffn_w1, ffn_w3, ffn_w2, pe_w, pe_gate_w, norm_final):
    W = dict(norm_mix=norm_mix, w_in=w_in, gdn_conv_w=gdn_conv_w, gdn_a_log=gdn_a_log, gdn_dt_bias=gdn_dt_bias,
             gdn_norm=gdn_norm, s5_lam_re=s5_lam_re, s5_lam_im=s5_lam_im, s5_log_dt=s5_log_dt, s5_b_re=s5_b_re,
             s5_b_im=s5_b_im, s5_c_re=s5_c_re, s5_c_im=s5_c_im, s5_d=s5_d, s5_glu_w=s5_glu_w, s5_glu_b=s5_glu_b,
             cc_dw_w=cc_dw_w, cc_dw_b=cc_dw_b, cc_ln_g=cc_ln_g, cc_ln_b=cc_ln_b, w_out=w_out, norm_ffn=norm_ffn,
             ffn_w1=ffn_w1, ffn_w3=ffn_w3, ffn_w2=ffn_w2, pe_w=pe_w, pe_gate_w=pe_gate_w, norm_final=norm_final)
    bp = x_prompt.shape[0]
    z_gdn = jnp.zeros((DEPTH, bp, GDN_HEADS, GDN_HEAD_DIM, GDN_HEAD_DIM), F32)
    z_gconv = jnp.zeros((DEPTH, bp, GDN_CONV - 1, QKV_COLS), x_prompt.dtype)
    z_s5 = jnp.zeros((DEPTH, bp, S5_GROUPS, S5_STATE, 2), F32)
    z_conv = jnp.zeros((DEPTH, bp, CC_KERNEL - 1, CC_WIDTH), x_prompt.dtype)
    y_prompt, gdn_p, gconv_p, s5_p, conv_p = trunk(x_prompt, p_prompt, z_gdn, z_gconv, z_s5, z_conv, W)
    y_sample, gdn_s, gconv_s, s5_s, conv_s = trunk(x_sample, p_sample, state_gdn, state_gdn_conv,
                                                   state_s5, state_conv, W)
    return (y_prompt, y_sample, gdn_p, gconv_p, s5_p, conv_p, gdn_s, gconv_s, s5_s, conv_s)
```

```python
import functools
import math

import jax
import jax.numpy as jnp
from jax import lax
from jax.experimental import pallas as pl
from jax.experimental.pallas import tpu as pltpu

F32 = jnp.float32
BF16 = jnp.bfloat16

D_MODEL = 1024
GDN_HEADS = 4
GDN_HEAD_DIM = 128
GDN_WIDTH = GDN_HEADS * GDN_HEAD_DIM
GDN_CONV = 4
QKV_COLS = 3 * GDN_WIDTH
S5_WIDTH = 256
S5_GROUP = 16
S5_GROUPS = 16
S5_STATE = 64
S5_CHUNK = 16
CC_WIDTH = 256
CC_KERNEL = 31
FFN_HIDDEN = 2816
FFN_CHUNK = 256
PLE_DIM = 256
OFF_Z = QKV_COLS
OFF_BA = OFF_Z + GDN_WIDTH
OFF_S5 = OFF_BA + 2 * GDN_HEADS
OFF_CC = OFF_S5 + S5_WIDTH
IN_COLS = OFF_CC + 2 * CC_WIDTH
LANES = 128
VMEM_LIMIT = 56 * 1024 * 1024


def _cparams(sem):
    return pltpu.CompilerParams(dimension_semantics=sem, vmem_limit_bytes=VMEM_LIMIT)


def _const_spec(shape):
    nd = len(shape)
    return pl.BlockSpec(shape, lambda *_: (0,) * nd, pipeline_mode=pl.Buffered(1))


def _split2(x):
    hi = x.astype(BF16)
    lo = (x - hi.astype(F32)).astype(BF16)
    return hi, lo


def _split3(x):
    hi = x.astype(BF16)
    r = x - hi.astype(F32)
    mid = r.astype(BF16)
    lo = (r - mid.astype(F32)).astype(BF16)
    return hi, mid, lo


def _dot(a, b):
    return jnp.dot(a, b, preferred_element_type=F32)


def _dot_nt(a, b):
    return lax.dot_general(a, b, (((1,), (1,)), ((), ())), preferred_element_type=F32)


def _dot_tn(a, b):
    return lax.dot_general(a, b, (((0,), (0,)), ((), ())), preferred_element_type=F32)


def _dot_x3(a, b):
    ah, al = _split2(a)
    bh, bl = _split2(b)
    return _dot(ah, bh) + (_dot(ah, bl) + _dot(al, bh))


def _dot_exact_lhs(l_bf16, x):
    h, m, lo = _split3(x)
    return _dot(l_bf16, h) + (_dot(l_bf16, m) + _dot(l_bf16, lo))


def _silu(x):
    return x * jax.nn.sigmoid(x)


def _inproj_kernel(h_ref, g_ref, wq_ref, wz_ref, wba_ref, wu_ref, wc_ref,
                   qkv_ref, z_ref, ba_ref, u5_ref, cc_ref):
    x = h_ref[...]
    ms = jnp.mean(x * x, axis=-1, keepdims=True)
    hn = (x * lax.rsqrt(ms + 1e-6)) * g_ref[...]
    hb = hn.astype(BF16)
    qkv_ref[...] = _dot(hb, wq_ref[...])
    z_ref[...] = _dot(hb, wz_ref[...])
    ba_ref[...] = _dot(hb, wba_ref[...])
    u5_ref[...] = _dot(hb, wu_ref[...])
    cc_ref[...] = _dot(hb, wc_ref[...])


def _in_proj(h, g, wq, wz, wba, wu, wc, tm):
    n = h.shape[0]
    row = lambda w: pl.BlockSpec((tm, w), lambda i: (i, 0))
    return pl.pallas_call(
        _inproj_kernel,
        grid=(n // tm,),
        in_specs=[row(D_MODEL), _const_spec((1, D_MODEL)), _const_spec(wq.shape), _const_spec(wz.shape),
                  _const_spec(wba.shape), _const_spec(wu.shape), _const_spec(wc.shape)],
        out_specs=[row(QKV_COLS), row(GDN_WIDTH), row(LANES), row(S5_WIDTH), row(2 * CC_WIDTH)],
        out_shape=[jax.ShapeDtypeStruct((n, QKV_COLS), F32), jax.ShapeDtypeStruct((n, GDN_WIDTH), F32),
                   jax.ShapeDtypeStruct((n, LANES), F32), jax.ShapeDtypeStruct((n, S5_WIDTH), F32),
                   jax.ShapeDtypeStruct((n, 2 * CC_WIDTH), F32)],
        compiler_params=_cparams(("parallel",)),
        name="in_proj",
    )(h, g, wq, wz, wba, wu, wc)


def _gdn_kernel(qkv_ref, z_ref, ba_ref, buf_ref, s0_ref, cw_ref, ad_ref, nw_ref,
                ya_ref, sn_ref, nb_ref,
                xbuf, s_scr, q_scr, k_scr, v_scr, gb_scr, *, tt, chunk):
    ti = pl.program_id(1)
    nt = pl.num_programs(1)
    hd = GDN_HEAD_DIM
    nh = GDN_HEADS
    c = chunk
    pad = 8
    hist = GDN_CONV - 1

    @pl.when(ti == 0)
    def _():
        xbuf[pad - hist:pad, :] = buf_ref[0]
        s_scr[...] = s0_ref[0]

    @pl.when(ti > 0)
    def _():
        xbuf[pad - hist:pad, :] = xbuf[pad + tt - hist:pad + tt, :]

    xbuf[pad:pad + tt, :] = qkv_ref[0]
    nb_ref[0] = xbuf[pad + tt - hist:pad + tt, :]

    acc = cw_ref[0:1, :] * xbuf[pad - hist:pad - hist + tt, :]
    for j in range(1, GDN_CONV):
        acc = acc + cw_ref[j:j + 1, :] * xbuf[pad - hist + j:pad - hist + j + tt, :]
    qkv = _silu(acc)

    for h in range(nh):
        qh = qkv[:, h * hd:(h + 1) * hd]
        kh = qkv[:, GDN_WIDTH + h * hd:GDN_WIDTH + (h + 1) * hd]
        qh = qh * (lax.rsqrt(jnp.sum(qh * qh, axis=-1, keepdims=True) + 1e-6) * (hd ** -0.5))
        kh = kh * lax.rsqrt(jnp.sum(kh * kh, axis=-1, keepdims=True) + 1e-6)
        q_scr[:, h * hd:(h + 1) * hd] = qh
        k_scr[:, h * hd:(h + 1) * hd] = kh
    v_scr[...] = qkv[:, 2 * GDN_WIDTH:]

    ba = ba_ref[0]
    beta = jax.nn.sigmoid(ba)
    sp_in = ba + ad_ref[1:2, :]
    softplus = jnp.maximum(sp_in, 0.0) + jnp.log(1.0 + jnp.exp(-jnp.abs(sp_in)))
    g = -jnp.exp(ad_ref[0:1, :]) * softplus
    gb_scr[0] = beta
    gb_scr[1] = g

    r_i = lax.broadcasted_iota(jnp.int32, (c, c), 0)
    c_i = lax.broadcasted_iota(jnp.int32, (c, c), 1)
    tril_incl = (r_i >= c_i).astype(BF16)
    pr = lax.broadcasted_iota(jnp.int32, (c, nh * c), 0)
    pc = lax.broadcasted_iota(jnp.int32, (c, nh * c), 1)
    ps = pc % c
    p_head = pc // c
    strict4 = pr > ps
    causal4 = pr >= ps
    eye4 = (pr == ps).astype(F32)
    bd_r = lax.broadcasted_iota(jnp.int32, (nh * c, nh * c), 0) // c
    bd_c = lax.broadcasted_iota(jnp.int32, (nh * c, nh * c), 1) // c
    bd_mask = bd_r == bd_c
    kbd_r = lax.broadcasted_iota(jnp.int32, (nh * c, GDN_WIDTH), 0) // c
    kbd_c = lax.broadcasted_iota(jnp.int32, (nh * c, GDN_WIDTH), 1) // hd
    kbd_mask = kbd_r == kbd_c
    n_dbl = int(math.log2(c)) - 1

    def blockdiag(x4):
        return jnp.where(bd_mask, jnp.concatenate([x4] * nh, axis=0), jnp.zeros((), x4.dtype))

    def mm_packed(a4, b4):
        ah, al = _split2(a4)
        bh, bl = _split2(b4)
        bdh = blockdiag(bh)
        bdl = blockdiag(bl)
        return _dot(ah, bdh) + (_dot(ah, bdl) + _dot(al, bdh))

    def pad_rows(x, h):
        z = jnp.zeros_like(x)
        return jnp.concatenate([x if j == h else z for j in range(nh)], axis=0)

    def chunk_body(ci, carry):
        r0 = pl.multiple_of(ci * c, c)
        rows = pl.ds(r0, c)
        q = q_scr[rows, :]
        k = k_scr[rows, :]
        v = v_scr[rows, :]
        beta_c = gb_scr[0, rows, :]
        g_c = gb_scr[1, rows, :]
        gc = _dot_exact_lhs(tril_incl, g_c)
        g_last = gc[c - 1:c, :]
        e_gc = jnp.exp(gc)
        e_rem = jnp.exp(g_last - gc)

        g4 = jnp.zeros((c, nh * c), F32)
        for h in range(nh):
            g4 = jnp.where(p_head == h, jnp.broadcast_to(g_c[:, nh + h:nh + h + 1], (c, nh * c)), g4)
        diff4 = _dot_exact_lhs(tril_incl, jnp.where(strict4, g4, 0.0))
        decay4 = jnp.exp(diff4)

        kb_parts, qg_parts, kd_parts, rhs_parts = [], [], [], []
        for h in range(nh):
            sl = slice(h * hd, (h + 1) * hd)
            b_h = jnp.broadcast_to(beta_c[:, h:h + 1], (c, hd))
            eg_h = jnp.broadcast_to(e_gc[:, nh + h:nh + h + 1], (c, hd))
            er_h = jnp.broadcast_to(e_rem[:, nh + h:nh + h + 1], (c, hd))
            kb_h = k[:, sl] * b_h
            kb_parts.append(kb_h)
            qg_parts.append(q[:, sl] * eg_h)
            kd_parts.append(k[:, sl] * er_h)
            rhs_parts.append(jnp.concatenate([v[:, sl] * b_h, kb_h * eg_h], axis=1))
        kb = jnp.concatenate(kb_parts, axis=1)

        k_bd = jnp.where(kbd_mask, jnp.concatenate([k] * nh, axis=0), 0.0).astype(BF16)
        kq = _dot_nt(jnp.concatenate([kb, q], axis=0).astype(BF16), k_bd)
        n4 = jnp.where(strict4, kq[0:c] * decay4, 0.0)
        aqk4 = jnp.where(causal4, kq[c:2 * c] * decay4, 0.0).astype(BF16)

        x4 = n4
        t4 = eye4 - n4
        for _ in range(n_dbl):
            x4 = mm_packed(x4, x4)
            t4 = t4 + mm_packed(t4, x4)
        t4b = t4.astype(BF16)

        e_last = jnp.exp(g_last)
        for h in range(nh):
            sl = slice(h * hd, (h + 1) * hd)
            sol = _dot(t4b, pad_rows(rhs_parts[h], h).astype(BF16))
            u_h = sol[:, 0:hd]
            w_h = sol[:, hd:2 * hd]
            s_h = s_scr[h]
            s_b = s_h.astype(BF16)
            v_new = u_h - _dot(w_h.astype(BF16), s_b)
            v_new_b = v_new.astype(BF16)
            o_h = _dot(qg_parts[h].astype(BF16), s_b) + _dot(aqk4, pad_rows(v_new_b, h))
            dl = jnp.broadcast_to(e_last[:, nh + h:nh + h + 1], (hd, hd))
            s_scr[h] = s_h * dl + _dot_tn(kd_parts[h].astype(BF16), v_new_b)
            o_n = o_h * lax.rsqrt(jnp.mean(o_h * o_h, axis=-1, keepdims=True) + 1e-6) * nw_ref[...]
            ya_ref[0, rows, sl] = o_n * _silu(z_ref[0, rows, sl])
        return carry

    lax.fori_loop(0, tt // c, chunk_body, 0)

    @pl.when(ti == nt - 1)
    def _():
        sn_ref[0] = s_scr[...]


def _gdn(qkv, z, ba, buf, s0, conv_w, ad, norm_w, tt, chunk):
    b, t, _ = qkv.shape
    kern = functools.partial(_gdn_kernel, tt=tt, chunk=chunk)
    tile = lambda w: pl.BlockSpec((1, tt, w), lambda i, j: (i, j, 0))
    per_b3 = lambda s: pl.BlockSpec((1,) + s, lambda i, j: (i,) + (0,) * len(s))
    return pl.pallas_call(
        kern,
        grid=(b, t // tt),
        in_specs=[tile(QKV_COLS), tile(GDN_WIDTH), tile(LANES), per_b3((GDN_CONV - 1, QKV_COLS)),
                  per_b3((GDN_HEADS, GDN_HEAD_DIM, GDN_HEAD_DIM)), _const_spec(conv_w.shape),
                  _const_spec(ad.shape), _const_spec(norm_w.shape)],
        out_specs=[tile(GDN_WIDTH), per_b3((GDN_HEADS, GDN_HEAD_DIM, GDN_HEAD_DIM)),
                   per_b3((GDN_CONV - 1, QKV_COLS))],
        out_shape=[jax.ShapeDtypeStruct((b, t, GDN_WIDTH), F32),
                   jax.ShapeDtypeStruct((b, GDN_HEADS, GDN_HEAD_DIM, GDN_HEAD_DIM), F32),
                   jax.ShapeDtypeStruct((b, GDN_CONV - 1, QKV_COLS), F32)],
        scratch_shapes=[pltpu.VMEM((tt + 8, QKV_COLS), F32),
                        pltpu.VMEM((GDN_HEADS, GDN_HEAD_DIM, GDN_HEAD_DIM), F32),
                        pltpu.VMEM((tt, GDN_WIDTH), F32), pltpu.VMEM((tt, GDN_WIDTH), F32),
                        pltpu.VMEM((tt, GDN_WIDTH), F32), pltpu.VMEM((2, tt, LANES), F32)],
        compiler_params=_cparams(("parallel", "arbitrary")),
        name="gdn",
    )(qkv, z, ba, buf, s0, conv_w, ad, norm_w)


def _s5_prep_kernel(lam_ref, ldt_ref, bt_ref, cm_ref, ktoep_ref, bend_ref, cst_ref, lamc_ref):
    g_n, p_n, lc = S5_GROUPS, S5_STATE, S5_CHUNK
    lr, li = lam_ref[0], lam_ref[1]
    dt = jnp.exp(ldt_ref[...])
    mag = jnp.exp(lr * dt)
    ang = li * dt
    br, bi = mag * jnp.cos(ang), mag * jnp.sin(ang)
    den = lr * lr + li * li
    qr = ((br - 1.0) * lr + bi * li) / den
    qi = (bi * lr - (br - 1.0) * li) / den
    bbr = qr * bt_ref[0] - qi * bt_ref[1]
    bbi = qr * bt_ref[1] + qi * bt_ref[0]
    cr, cim = cm_ref[0], cm_ref[1]

    pw = [(jnp.ones_like(br), jnp.zeros_like(br))]
    for _ in range(lc):
        ar, ai = pw[-1]
        pw.append((ar * br - ai * bi, ar * bi + ai * br))
    lamc_ref[:, :, 0:p_n] = pw[lc][0]
    lamc_ref[:, :, p_n:2 * p_n] = pw[lc][1]

    for s in range(lc):
        ar, ai = pw[lc - 1 - s]
        bend_ref[:, s, :, 0:p_n] = ar * bbr - ai * bbi
        bend_ref[:, s, :, p_n:2 * p_n] = ar * bbi + ai * bbr
        ar, ai = pw[s + 1]
        cst_ref[:, s, :, 0:p_n] = ar * cr - ai * cim
        cst_ref[:, s, :, p_n:2 * p_n] = -(ar * cim + ai * cr)

    lane = lax.broadcasted_iota(jnp.int32, (S5_GROUP, lc * S5_GROUP), 1)
    for g in range(g_n):
        cpr = jnp.concatenate([pw[j][0][g] * cr[g] - pw[j][1][g] * cim[g] for j in range(lc)], axis=0)
        cpi = jnp.concatenate([pw[j][0][g] * cim[g] + pw[j][1][g] * cr[g] for j in range(lc)], axis=0)
        bh_r, bl_r = _split2(bbr[g])
        bh_i, bl_i = _split2(bbi[g])
        ch_r, cl_r = _split2(cpr)
        ch_i, cl_i = _split2(cpi)
        krow = (_dot_nt(bh_r, ch_r) + (_dot_nt(bh_r, cl_r) + _dot_nt(bl_r, ch_r))
                - (_dot_nt(bh_i, ch_i) + (_dot_nt(bh_i, cl_i) + _dot_nt(bl_i, ch_i))))
        for s in range(lc):
            sh = s * S5_GROUP
            blk = krow if s == 0 else jnp.where(lane >= sh, pltpu.roll(krow, sh, 1), 0.0)
            ktoep_ref[g, s * S5_GROUP:(s + 1) * S5_GROUP, :] = blk.astype(BF16)


def _s5_prep(lam, ldt, bt, cm):
    g_n, p_n, lc = S5_GROUPS, S5_STATE, S5_CHUNK
    w = lc * S5_GROUP
    return pl.pallas_call(
        _s5_prep_kernel,
        out_shape=[jax.ShapeDtypeStruct((g_n, w, w), BF16),
                   jax.ShapeDtypeStruct((g_n, lc, S5_GROUP, 2 * p_n), F32),
                   jax.ShapeDtypeStruct((g_n, lc, S5_GROUP, 2 * p_n), F32),
                   jax.ShapeDtypeStruct((g_n, 1, 2 * p_n), F32)],
        compiler_params=pltpu.CompilerParams(vmem_limit_bytes=VMEM_LIMIT),
        name="s5_prep",
    )(lam, ldt, bt, cm)


def _s5_kernel(u_ref, h0_ref, ktoep_ref, bend_ref, cst_ref, lamc_ref, d_ref, y_ref, hn_ref, *, n):
    p_n = S5_STATE
    lane = lax.broadcasted_iota(jnp.int32, (1, 2 * p_n), 1)
    sign = jnp.where(lane < p_n, -1.0, 1.0)
    row = lax.broadcasted_iota(jnp.int32, (n, 2 * p_n), 0)

    def cmul(x, ar_ai):
        a_rr, a_is = ar_ai
        return x * a_rr + pltpu.roll(x, p_n, 1) * a_is

    def group_body(g, carry):
        u = u_ref[0, g]
        lam = lamc_ref[pl.ds(g, 1), :]
        lam_sw = pltpu.roll(lam, p_n, 1)
        a_rr = jnp.where(lane < p_n, lam, lam_sw)
        a_is = jnp.where(lane < p_n, lam_sw, lam) * sign

        x_loc = _dot_x3(u, bend_ref[g])
        h0 = h0_ref[0, pl.ds(g, 1), :]
        if n > 1:
            xs = jnp.where(row == 0, h0, pltpu.roll(x_loc, 1, 0))
        else:
            xs = h0
        step = 1
        while step < n:
            sh = jnp.where(row >= step, pltpu.roll(xs, step, 0), 0.0)
            xs = xs + cmul(sh, (a_rr, a_is))
            n_rr = a_rr * a_rr - a_is * a_is
            n_is = 2.0 * a_rr * a_is
            a_rr, a_is = n_rr, n_is
            step *= 2
        y = _dot(u.astype(BF16), ktoep_ref[g]) + _dot_nt(xs.astype(BF16), cst_ref[g].astype(BF16))
        y = y + d_ref[pl.ds(g, 1), :] * u
        y_ref[0, g] = jax.nn.gelu(y)
        lam = lamc_ref[pl.ds(g, 1), :]
        lam_sw = pltpu.roll(lam, p_n, 1)
        b_rr = jnp.where(lane < p_n, lam, lam_sw)
        b_is = jnp.where(lane < p_n, lam_sw, lam) * sign
        x_last = xs[n - 1:n, :]
        hn_ref[0, pl.ds(g, 1), :] = x_last * b_rr + pltpu.roll(x_last, p_n, 1) * b_is + x_loc[n - 1:n, :]
        return carry

    lax.fori_loop(0, S5_GROUPS, group_body, 0)


def _s5(u, h0, ktoep, bend, cst, lamc, d_t):
    b, g_n, n, w = u.shape
    p2 = 2 * S5_STATE
    kern = functools.partial(_s5_kernel, n=n)
    return pl.pallas_call(
        kern,
        grid=(b,),
        in_specs=[pl.BlockSpec((1, g_n, n, w), lambda i: (i, 0, 0, 0)),
                  pl.BlockSpec((1, g_n, p2), lambda i: (i, 0, 0)),
                  _const_spec(ktoep.shape), _const_spec(bend.shape), _const_spec(cst.shape),
                  _const_spec(lamc.shape), _const_spec(d_t.shape)],
        out_specs=[pl.BlockSpec((1, g_n, n, w), lambda i: (i, 0, 0, 0)),
                   pl.BlockSpec((1, g_n, p2), lambda i: (i, 0, 0))],
        out_shape=[jax.ShapeDtypeStruct((b, g_n, n, w), F32), jax.ShapeDtypeStruct((b, g_n, p2), F32)],
        compiler_params=_cparams(("parallel",)),
        name="s5",
    )(u, h0, ktoep, bend, cst, lamc, d_t)


def _cc_kernel(cc_ref, buf_ref, w_ref, p_ref, yc_ref, nb_ref, xbuf, *, tt):
    ti = pl.program_id(1)
    hist = CC_KERNEL - 1
    pad = 32

    @pl.when(ti == 0)
    def _():
        xbuf[pad - hist:pad, :] = buf_ref[0]

    @pl.when(ti > 0)
    def _():
        xbuf[pad - hist:pad, :] = xbuf[pad + tt - hist:pad + tt, :]

    cc = cc_ref[0]
    xbuf[pad:pad + tt, :] = cc[:, 0:CC_WIDTH] * jax.nn.sigmoid(cc[:, CC_WIDTH:])
    nb_ref[0] = xbuf[pad + tt - hist:pad + tt, :]

    acc = w_ref[0:1, :] * xbuf[pad - hist:pad - hist + tt, :]
    for j in range(1, CC_KERNEL):
        acc = acc + w_ref[j:j + 1, :] * xbuf[pad - hist + j:pad - hist + j + tt, :]
    acc = acc + p_ref[0:1, :]
    mu = jnp.mean(acc, axis=-1, keepdims=True)
    xc = acc - mu
    var = jnp.mean(xc * xc, axis=-1, keepdims=True)
    yc_ref[0] = _silu(xc * lax.rsqrt(var + 1e-5) * p_ref[1:2, :] + p_ref[2:3, :])


def _cc(cc, buf, dw_w, params, tt):
    b, t, _ = cc.shape
    kern = functools.partial(_cc_kernel, tt=tt)
    return pl.pallas_call(
        kern,
        grid=(b, t // tt),
        in_specs=[pl.BlockSpec((1, tt, 2 * CC_WIDTH), lambda i, j: (i, j, 0)),
                  pl.BlockSpec((1, CC_KERNEL - 1, CC_WIDTH), lambda i, j: (i, 0, 0)),
                  _const_spec(dw_w.shape), _const_spec(params.shape)],
        out_specs=[pl.BlockSpec((1, tt, CC_WIDTH), lambda i, j: (i, j, 0)),
                   pl.BlockSpec((1, CC_KERNEL - 1, CC_WIDTH), lambda i, j: (i, 0, 0))],
        out_shape=[jax.ShapeDtypeStruct((b, t, CC_WIDTH), F32),
                   jax.ShapeDtypeStruct((b, CC_KERNEL - 1, CC_WIDTH), F32)],
        scratch_shapes=[pltpu.VMEM((tt + 32, CC_WIDTH), F32)],
        compiler_params=_cparams(("parallel", "arbitrary")),
        name="cc",
    )(cc, buf, dw_w, params)


def _out_ffn_kernel(h_ref, ya_ref, yb_ref, yc_ref, p_ref, gw_ref, gb_ref, woa_ref, wob_ref, woc_ref,
                    nf_ref, w1_ref, w3_ref, w2_ref, pew_ref, peg_ref, nfin_ref, o_ref, *, final):
    yb = yb_ref[...]
    yb = yb * jax.nn.sigmoid(_dot(yb.astype(BF16), gw_ref[...]) + gb_ref[...])
    h = h_ref[...] + (_dot(ya_ref[...].astype(BF16), woa_ref[...])
                      + _dot(yb.astype(BF16), wob_ref[...])
                      + _dot(yc_ref[...].astype(BF16), woc_ref[...]))
    ms = jnp.mean(h * h, axis=-1, keepdims=True)
    hf = ((h * lax.rsqrt(ms + 1e-6)) * nf_ref[...]).astype(BF16)

    def ffn_body(j, acc):
        a = _dot(hf, w1_ref[j])
        b = _dot(hf, w3_ref[j])
        return acc + _dot((_silu(a) * b).astype(BF16), w2_ref[j])

    h = h + lax.fori_loop(0, FFN_HIDDEN // FFN_CHUNK, ffn_body, jnp.zeros_like(h))
    pe = _dot(p_ref[...].astype(BF16), pew_ref[...])
    h = h + pe * jax.nn.sigmoid(_dot(h.astype(BF16), peg_ref[...]))
    if final:
        ms = jnp.mean(h * h, axis=-1, keepdims=True)
        h = (h * lax.rsqrt(ms + 1e-6)) * nfin_ref[...]
    o_ref[...] = h


def _out_ffn(h, ya, yb, yc, p, gw, gb, woa, wob, woc, nf, w1, w3, w2, pew, peg, nfin, tm, final):
    n = h.shape[0]
    row = lambda w: pl.BlockSpec((tm, w), lambda i: (i, 0))
    consts = [gw, gb, woa, wob, woc, nf, w1, w3, w2, pew, peg, nfin]
    return pl.pallas_call(
        functools.partial(_out_ffn_kernel, final=final),
        grid=(n // tm,),
        in_specs=[row(D_MODEL), row(GDN_WIDTH), row(S5_WIDTH), row(CC_WIDTH), row(PLE_DIM)]
        + [_const_spec(a.shape) for a in consts],
        out_specs=row(D_MODEL),
        out_shape=jax.ShapeDtypeStruct((n, D_MODEL), F32),
        compiler_params=_cparams(("parallel",)),
        name="out_ffn",
    )(h, ya, yb, yc, p, *consts)


def _tile_rows(n, want):
    t = min(n, want)
    while n % t:
        t //= 2
    return t


def _prep_layer(i, W):
    w_in = W['w_in'][i]
    L = {}
    L['g_mix'] = W['norm_mix'][i][None, :]
    L['wq'] = w_in[:, :OFF_Z].astype(BF16)
    L['wz'] = w_in[:, OFF_Z:OFF_BA].astype(BF16)
    L['wba'] = jnp.pad(w_in[:, OFF_BA:OFF_S5], ((0, 0), (0, LANES - 2 * GDN_HEADS))).astype(BF16)
    L['wu'] = w_in[:, OFF_S5:OFF_CC].astype(BF16)
    L['wc'] = w_in[:, OFF_CC:].astype(BF16)
    L['conv_w'] = W['gdn_conv_w'][i]
    padl = (GDN_HEADS, LANES - 2 * GDN_HEADS)
    L['ad'] = jnp.stack([jnp.pad(W['gdn_a_log'][i], padl), jnp.pad(W['gdn_dt_bias'][i], padl)])
    L['gdn_norm'] = W['gdn_norm'][i][None, :]
    lam = jnp.stack([W['s5_lam_re'][i], W['s5_lam_im'][i]])[:, :, None, :]
    bt =jnp.stack([jnp.swapaxes(W['s5_b_re'][i], 1, 2), jnp.swapaxes(W['s5_b_im'][i], 1, 2)])
    cm = jnp.stack([W['s5_c_re'][i], W['s5_c_im'][i]])
    ktoep, bend, cst, lamc = _s5_prep(lam, W['s5_log_dt'][i][:, None, None], bt, cm)
    w = S5_CHUNK * S5_GROUP
    L['ktoep'] = ktoep
    L['bend'] = bend.reshape(S5_GROUPS, w, 2 * S5_STATE)
    L['cst'] = cst.reshape(S5_GROUPS, w, 2 * S5_STATE)
    L['lamc'] = lamc.reshape(S5_GROUPS, 2 * S5_STATE)
    L['d_t'] = jnp.tile(W['s5_d'][i].reshape(S5_GROUPS, S5_GROUP), (1, S5_CHUNK))
    L['glu_w'] = W['s5_glu_w'][i].astype(BF16)
    L['glu_b'] = W['s5_glu_b'][i][None, :]
    L['dw_w'] = W['cc_dw_w'][i]
    L['cc_p'] = jnp.stack([W['cc_dw_b'][i], W['cc_ln_g'][i], W['cc_ln_b'][i]])
    wo = W['w_out'][i].astype(BF16)
    L['woa'], L['wob'], L['woc'] = wo[:GDN_WIDTH], wo[GDN_WIDTH:GDN_WIDTH + S5_WIDTH], wo[GDN_WIDTH + S5_WIDTH:]
    L['g_ffn'] = W['norm_ffn'][i][None, :]
    nck = FFN_HIDDEN // FFN_CHUNK
    L['w1'] = jnp.swapaxes(W['ffn_w1'][i].astype(BF16).reshape(D_MODEL, nck, FFN_CHUNK), 0, 1)
    L['w3'] = jnp.swapaxes(W['ffn_w3'][i].astype(BF16).reshape(D_MODEL, nck, FFN_CHUNK), 0, 1)
    L['w2'] = W['ffn_w2'][i].astype(BF16).reshape(nck, FFN_CHUNK, D_MODEL)
    L['pew'] = W['pe_w'][i].astype(BF16)
    L['peg'] = W['pe_gate_w'][i].astype(BF16)
    L['g_fin'] = W['norm_final'][None, :]
    return L


def _layer(h, p, st_gdn, st_gconv, st_s5, st_conv, L, final):
    b, t, _ = h.shape
    n = b * t
    tm = _tile_rows(n, 512)
    qkv, z, ba, u5, cc = _in_proj(h.reshape(n, D_MODEL), L['g_mix'], L['wq'], L['wz'], L['wba'], L['wu'], L['wc'], tm)

    chunk = 64 if t % 64 == 0 else t
    tt = _tile_rows(t, 256)
    ya, s_gdn, b_gconv = _gdn(qkv.reshape(b, t, QKV_COLS), z.reshape(b, t, GDN_WIDTH), ba.reshape(b, t, LANES),
                              st_gconv, st_gdn, L['conv_w'], L['ad'], L['gdn_norm'], tt, chunk)

    nck = t // S5_CHUNK
    u_f = jnp.transpose(u5.reshape(b, nck, S5_CHUNK, S5_GROUPS, S5_GROUP), (0, 3, 1, 2, 4))
    u_f = u_f.reshape(b, S5_GROUPS, nck, S5_CHUNK * S5_GROUP)
    h0 = jnp.concatenate([st_s5[..., 0], st_s5[..., 1]], axis=-1)
    y_f, hn = _s5(u_f, h0, L['ktoep'], L['bend'], L['cst'], L['lamc'], L['d_t'])
    yb = jnp.transpose(y_f.reshape(b, S5_GROUPS, nck, S5_CHUNK, S5_GROUP), (0, 2, 3, 1, 4)).reshape(n, S5_WIDTH)
    s_s5 = jnp.stack([hn[..., :S5_STATE], hn[..., S5_STATE:]], axis=-1)

    yc, b_conv = _cc(cc.reshape(b, t, 2 * CC_WIDTH), st_conv, L['dw_w'], L['cc_p'], _tile_rows(t, 512))

    h_new = _out_ffn(h.reshape(n, D_MODEL), ya.reshape(n, GDN_WIDTH), yb, yc.reshape(n, CC_WIDTH),
                     p.reshape(n, PLE_DIM), L['glu_w'], L['glu_b'], L['woa'], L['wob'], L['woc'], L['g_ffn'],
                     L['w1'], L['w3'], L['w2'], L['pew'], L['peg'], L['g_fin'], tm, final)
    return h_new.reshape(b, t, D_MODEL), s_gdn, b_gconv, s_s5, b_conv


def _trunk(x, p, st_gdn, st_gconv, st_s5, st_conv, layers):
    h = x
    outs = ([], [], [], [])
    depth = len(layers)
    for i, L in enumerate(layers):
        h, *st = _layer(h, p[i], st_gdn[i], st_gconv[i], st_s5[i], st_conv[i], L, i == depth - 1)
        for acc, s in zip(outs, st):
            acc.append(s)
    return (h,) + tuple(jnp.stack(o) for o in outs)


def kernel(x_prompt, x_sample, p_prompt, p_sample, state_gdn, state_gdn_conv, state_s5, state_conv, norm_mix, w_in, gdn_conv_w, gdn_a_log, gdn_dt_bias, gdn_norm, s5_lam_re, s5_lam_im, s5_log_dt, s5_b_re, s5_b_im, s5_c_re, s5_c_im, s5_d, s5_glu_w, s5_glu_b, cc_dw_w, cc_dw_b, cc_ln_g, cc_ln_b, w_out, norm_ffn, ffn_w1, ffn_w3, ffn_w2, pe_w, pe_gate_w, norm_final):
    W = dict(norm_mix=norm_mix, w_in=w_in, gdn_conv_w=gdn_conv_w, gdn_a_log=gdn_a_log, gdn_dt_bias=gdn_dt_bias,
             gdn_norm=gdn_norm, s5_lam_re=s5_lam_re, s5_lam_im=s5_lam_im, s5_log_dt=s5_log_dt, s5_b_re=s5_b_re,
             s5_b_im=s5_b_im, s5_c_re=s5_c_re, s5_c_im=s5_c_im, s5_d=s5_d, s5_glu_w=s5_glu_w, s5_glu_b=s5_glu_b,
             cc_dw_w=cc_dw_w, cc_dw_b=cc_dw_b, cc_ln_g=cc_ln_g, cc_ln_b=cc_ln_b, w_out=w_out, norm_ffn=norm_ffn,
             ffn_w1=ffn_w1, ffn_w3=ffn_w3, ffn_w2=ffn_w2, pe_w=pe_w, pe_gate_w=pe_gate_w, norm_final=norm_final)
    depth = w_in.shape[0]
    layers = [_prep_layer(i, W) for i in range(depth)]
    bp = x_prompt.shape[0]
    z_gdn = jnp.zeros((depth, bp, GDN_HEADS, GDN_HEAD_DIM, GDN_HEAD_DIM), F32)
    z_gconv = jnp.zeros((depth, bp, GDN_CONV - 1, QKV_COLS), F32)
    z_s5 = jnp.zeros((depth, bp, S5_GROUPS, S5_STATE, 2), F32)
    z_conv = jnp.zeros((depth, bp, CC_KERNEL - 1, CC_WIDTH), F32)
    y_p, gdn_p, gconv_p, s5_p, conv_p = _trunk(x_prompt, p_prompt, z_gdn, z_gconv, z_s5, z_conv, layers)
    y_s, gdn_s, gconv_s, s5_s, conv_s = _trunk(x_sample, p_sample, state_gdn, state_gdn_conv, state_s5,
                                               state_conv, layers)
    return (y_p, y_s, gdn_p, gconv_p, s5_p, conv_p, gdn_s, gconv_s, s5_s, conv_s)
```

```python
import functools
import math

import jax
import jax.numpy as jnp
from jax import lax
from jax.experimental import pallas as pl
from jax.experimental.pallas import tpu as pltpu

F32 = jnp.float32
BF16 = jnp.bfloat16

D_MODEL = 1024
GDN_HEADS = 4
GDN_HEAD_DIM = 128
GDN_WIDTH = GDN_HEADS * GDN_HEAD_DIM
GDN_CONV = 4
QKV_COLS = 3 * GDN_WIDTH
S5_WIDTH = 256
S5_GROUP = 16
S5_GROUPS = 16
S5_STATE = 64
S5_CHUNK = 16
S5_ROWS = 256
CC_WIDTH = 256
CC_KERNEL = 31
FFN_HIDDEN = 2816
FFN_CHUNK = 256
PLE_DIM = 256
OFF_Z = QKV_COLS
OFF_BA = OFF_Z + GDN_WIDTH
OFF_S5 = OFF_BA + 2 * GDN_HEADS
OFF_CC = OFF_S5 + S5_WIDTH
IN_COLS = OFF_CC + 2 * CC_WIDTH
LANES = 128
VMEM_LIMIT = 56 * 1024 * 1024


def _cparams(sem):
    return pltpu.CompilerParams(dimension_semantics=sem, vmem_limit_bytes=VMEM_LIMIT)


def _const_spec(shape):
    nd = len(shape)
    return pl.BlockSpec(shape, lambda *_: (0,) * nd, pipeline_mode=pl.Buffered(1))


def _split2(x):
    hi = x.astype(BF16)
    lo = (x - hi.astype(F32)).astype(BF16)
    return hi, lo


def _split3(x):
    hi = x.astype(BF16)
    r = x - hi.astype(F32)
    mid = r.astype(BF16)
    lo = (r - mid.astype(F32)).astype(BF16)
    return hi, mid, lo


def _dot(a, b):
    return jnp.dot(a, b, preferred_element_type=F32)


def _dot_nt(a, b):
    return lax.dot_general(a, b, (((1,), (1,)), ((), ())), preferred_element_type=F32)


def _dot_tn(a, b):
    return lax.dot_general(a, b, (((0,), (0,)), ((), ())), preferred_element_type=F32)


def _dot_x3(a, b):
    ah, al = _split2(a)
    bh, bl = _split2(b)
    return _dot(ah, bh) + (_dot(ah, bl) + _dot(al, bh))


def _dot_exact_lhs(l_bf16, x):
    h, m, lo = _split3(x)
    return _dot(l_bf16, h) + (_dot(l_bf16, m) + _dot(l_bf16, lo))


def _silu(x):
    return x * jax.nn.sigmoid(x)


def _inproj_kernel(h_ref, g_ref, wq_ref, wz_ref, wba_ref, wu_ref, wc_ref,
                   qkv_ref, z_ref, ba_ref, u5_ref, cc_ref):
    x = h_ref[...]
    ms = jnp.mean(x * x, axis=-1, keepdims=True)
    hn = (x * lax.rsqrt(ms + 1e-6)) * g_ref[...]
    hb = hn.astype(BF16)
    qkv_ref[...] = _dot(hb, wq_ref[...])
    z_ref[...] = _dot(hb, wz_ref[...])
    ba_ref[...] = _dot(hb, wba_ref[...])
    u5_ref[...] = _dot(hb, wu_ref[...])
    cc_ref[...] = _dot(hb, wc_ref[...])


def _in_proj(h, g, wq, wz, wba, wu, wc, tm):
    n = h.shape[0]
    row = lambda w: pl.BlockSpec((tm, w), lambda i: (i, 0))
    return pl.pallas_call(
        _inproj_kernel,
        grid=(n // tm,),
        in_specs=[row(D_MODEL), _const_spec((1, D_MODEL)), _const_spec(wq.shape), _const_spec(wz.shape),
                  _const_spec(wba.shape), _const_spec(wu.shape), _const_spec(wc.shape)],
        out_specs=[row(QKV_COLS), row(GDN_WIDTH), row(LANES), row(S5_WIDTH), row(2 * CC_WIDTH)],
        out_shape=[jax.ShapeDtypeStruct((n, QKV_COLS), F32), jax.ShapeDtypeStruct((n, GDN_WIDTH), F32),
                   jax.ShapeDtypeStruct((n, LANES), F32), jax.ShapeDtypeStruct((n, S5_WIDTH), F32),
                   jax.ShapeDtypeStruct((n, 2 * CC_WIDTH), F32)],
        compiler_params=_cparams(("parallel",)),
        name="in_proj",
    )(h, g, wq, wz, wba, wu, wc)


def _gdn_kernel(qkv_ref, z_ref, ba_ref, buf_ref, s0_ref, cw_ref, ad_ref, nw_ref,
                ya_ref, sn_ref, nb_ref,
                xbuf, s_scr, q_scr, k_scr, v_scr, gb_scr, *, tt, chunk):
    ti = pl.program_id(1)
    nt = pl.num_programs(1)
    hd = GDN_HEAD_DIM
    nh = GDN_HEADS
    c = chunk
    pad = 8
    hist = GDN_CONV - 1

    @pl.when(ti == 0)
    def _():
        xbuf[pad - hist:pad, :] = buf_ref[0]
        s_scr[...] = s0_ref[0]

    @pl.when(ti > 0)
    def _():
        xbuf[pad - hist:pad, :] = xbuf[pad + tt - hist:pad + tt, :]

    xbuf[pad:pad + tt, :] = qkv_ref[0]
    nb_ref[0] = xbuf[pad + tt - hist:pad + tt, :]

    acc = cw_ref[0:1, :] * xbuf[pad - hist:pad - hist + tt, :]
    for j in range(1, GDN_CONV):
        acc = acc + cw_ref[j:j + 1, :] * xbuf[pad - hist + j:pad - hist + j + tt, :]
    qkv = _silu(acc)

    for h in range(nh):
        qh = qkv[:, h * hd:(h + 1) * hd]
        kh = qkv[:, GDN_WIDTH + h * hd:GDN_WIDTH + (h + 1) * hd]
        qh = qh * (lax.rsqrt(jnp.sum(qh * qh, axis=-1, keepdims=True) + 1e-6) * (hd ** -0.5))
        kh = kh * lax.rsqrt(jnp.sum(kh * kh, axis=-1, keepdims=True) + 1e-6)
        q_scr[:, h * hd:(h + 1) * hd] = qh
        k_scr[:, h * hd:(h + 1) * hd] = kh
    v_scr[...] = qkv[:, 2 * GDN_WIDTH:]

    ba = ba_ref[0]
    beta = jax.nn.sigmoid(ba)
    sp_in = ba + ad_ref[1:2, :]
    softplus = jnp.maximum(sp_in, 0.0) + jnp.log(1.0 + jnp.exp(-jnp.abs(sp_in)))
    g = -jnp.exp(ad_ref[0:1, :]) * softplus
    gb_scr[0] = beta
    gb_scr[1] = g

    nck = tt // c
    t_r = lax.broadcasted_iota(jnp.int32, (tt, tt), 0)
    t_c = lax.broadcasted_iota(jnp.int32, (tt, tt), 1)
    tril_bd = ((t_r // c == t_c // c) & (t_r >= t_c)).astype(BF16)
    pi = lax.broadcasted_iota(jnp.int32, (tt, nh * c), 0) % c
    pc = lax.broadcasted_iota(jnp.int32, (tt, nh * c), 1)
    ps = pc % c
    p_head = pc // c
    ci_r = lax.broadcasted_iota(jnp.int32, (c, nh * c), 0)
    ci_s = lax.broadcasted_iota(jnp.int32, (c, nh * c), 1) % c
    strict4 = ci_r > ci_s
    causal4 = ci_r >= ci_s
    eye4 = (ci_r == ci_s).astype(F32)
    bd_r = lax.broadcasted_iota(jnp.int32, (nh * c, nh * c), 0) // c
    bd_c = lax.broadcasted_iota(jnp.int32, (nh * c, nh * c), 1) // c
    bd_mask = (bd_r == bd_c).astype(BF16)
    kbd_r = lax.broadcasted_iota(jnp.int32, (nh * c, GDN_WIDTH), 0) // c
    kbd_c = lax.broadcasted_iota(jnp.int32, (nh * c, GDN_WIDTH), 1) // hd
    kbd_mask = (kbd_r == kbd_c).astype(BF16)
    n_dbl = int(math.log2(c)) - 1

    def blockdiag(xb):
        return jnp.concatenate([xb] * nh, axis=0) * bd_mask

    def mm_packed(a4, b4):
        return _dot(a4.astype(BF16), blockdiag(b4.astype(BF16)))

    def pad_rows(x, h):
        z = jnp.zeros_like(x)
        return jnp.concatenate([x if j == h else z for j in range(nh)], axis=0)

    g_all = gb_scr[1]
    gc_all = _dot_exact_lhs(tril_bd, g_all)
    g4 = jnp.zeros((tt, nh * c), F32)
    for h in range(nh):
        g4 = jnp.where(p_head == h, jnp.broadcast_to(g_all[:, nh + h:nh + h + 1], (tt, nh * c)), g4)
    decay_all = jnp.exp(_dot_exact_lhs(tril_bd, jnp.where(pi > ps, g4, 0.0)))

    chunk_rows = [slice(ck * c, (ck + 1) * c) for ck in range(nck)]
    g_last_b = jnp.concatenate(
        [jnp.broadcast_to(gc_all[(ck + 1) * c - 1:(ck + 1) * c], (c, LANES)) for ck in range(nck)], axis=0)
    e_gc = jnp.exp(gc_all)
    e_rem = jnp.exp(g_last_b - gc_all)
    e_last = [jnp.exp(gc_all[(ck + 1) * c - 1:(ck + 1) * c]) for ck in range(nck)]
    beta_all = gb_scr[0]
    q_all = q_scr[...]
    k_all = k_scr[...]
    v_all = v_scr[...]
    kb_parts, wq_b, kd_b, rhs_b = [], [], [], []
    for h in range(nh):
        sl = slice(h * hd, (h + 1) * hd)
        b_h = jnp.broadcast_to(beta_all[:, h:h + 1], (tt, hd))
        eg_h = jnp.broadcast_to(e_gc[:, nh + h:nh + h + 1], (tt, hd))
        er_h = jnp.broadcast_to(e_rem[:, nh + h:nh + h + 1], (tt, hd))
        kb_h = k_all[:, sl] * b_h
        kb_parts.append(kb_h)
        wq_b.append((q_all[:, sl] * eg_h).astype(BF16))
        kd_b.append((k_all[:, sl] * er_h).astype(BF16))
        rhs_b.append(jnp.concatenate([v_all[:, sl] * b_h, kb_h * eg_h], axis=1).astype(BF16))
    kb_b = jnp.concatenate(kb_parts, axis=1).astype(BF16)
    q_b = q_all.astype(BF16)
    k_b = k_all.astype(BF16)

    kq = []
    for rows in chunk_rows:
        k_bd = jnp.concatenate([k_b[rows]] * nh, axis=0) * kbd_mask
        kq.append(_dot_nt(jnp.concatenate([kb_b[rows], q_b[rows]], axis=0), k_bd))
    x4, t4, aqk4 = [], [], []
    for ck, rows in enumerate(chunk_rows):
        n4 = jnp.where(strict4, kq[ck][0:c] * decay_all[rows], 0.0)
        aqk4.append(jnp.where(causal4, kq[ck][c:2 * c] * decay_all[rows], 0.0).astype(BF16))
        x4.append(n4)
        t4.append(eye4 - n4)
    for _ in range(n_dbl):
        x4 = [mm_packed(x, x) for x in x4]
        t4 = [t + mm_packed(t, x) for t, x in zip(t4, x4)]
    uw = []
    for ck, rows in enumerate(chunk_rows):
        t4b = t4[ck].astype(BF16)
        uw.append([_dot(t4b, pad_rows(rhs_b[h][rows], h)) for h in range(nh)])

    s_cur = [s_scr[h] for h in range(nh)]
    for ck, rows in enumerate(chunk_rows):
        ws = [_dot(jnp.concatenate([uw[ck][h][:, hd:2 * hd].astype(BF16), wq_b[h][rows]], axis=0),
                   s_cur[h].astype(BF16)) for h in range(nh)]
        v_new = [(uw[ck][h][:, 0:hd] - ws[h][0:c]).astype(BF16) for h in range(nh)]
        s_add = [_dot_tn(kd_b[h][rows], v_new[h]) for h in range(nh)]
        o_att = [_dot(aqk4[ck], pad_rows(v_new[h], h)) for h in range(nh)]
        for h in range(nh):
            sl = slice(h * hd, (h + 1) * hd)
            dl = jnp.broadcast_to(e_last[ck][:, nh + h:nh + h + 1], (hd, hd))
            s_cur[h] = s_cur[h] * dl + s_add[h]
            o_h = ws[h][c:2 * c] + o_att[h]
            o_n = o_h * lax.rsqrt(jnp.mean(o_h * o_h, axis=-1, keepdims=True) + 1e-6) * nw_ref[...]
            ya_ref[0, rows, sl] = o_n * _silu(z_ref[0, rows, sl])
    for h in range(nh):
        s_scr[h] = s_cur[h]

    @pl.when(ti == nt - 1)
    def _():
        sn_ref[0] = s_scr[...]


def _gdn(qkv, z, ba, buf, s0, conv_w, ad, norm_w, tt, chunk):
    b, t, _ = qkv.shape
    kern = functools.partial(_gdn_kernel, tt=tt, chunk=chunk)
    tile = lambda w: pl.BlockSpec((1, tt, w), lambda i, j: (i, j, 0))
    per_b3 = lambda s: pl.BlockSpec((1,) + s, lambda i, j: (i,) + (0,) * len(s))
    return pl.pallas_call(
        kern,
        grid=(b, t // tt),
        in_specs=[tile(QKV_COLS), tile(GDN_WIDTH), tile(LANES), per_b3((GDN_CONV - 1, QKV_COLS)),
                  per_b3((GDN_HEADS, GDN_HEAD_DIM, GDN_HEAD_DIM)), _const_spec(conv_w.shape),
                  _const_spec(ad.shape), _const_spec(norm_w.shape)],
        out_specs=[tile(GDN_WIDTH), per_b3((GDN_HEADS, GDN_HEAD_DIM, GDN_HEAD_DIM)),
                   per_b3((GDN_CONV - 1, QKV_COLS))],
        out_shape=[jax.ShapeDtypeStruct((b, t, GDN_WIDTH), F32),
                   jax.ShapeDtypeStruct((b, GDN_HEADS, GDN_HEAD_DIM, GDN_HEAD_DIM), F32),
                   jax.ShapeDtypeStruct((b, GDN_CONV - 1, QKV_COLS), F32)],
        scratch_shapes=[pltpu.VMEM((tt + 8, QKV_COLS), F32),
                        pltpu.VMEM((GDN_HEADS, GDN_HEAD_DIM, GDN_HEAD_DIM), F32),
                        pltpu.VMEM((tt, GDN_WIDTH), F32), pltpu.VMEM((tt, GDN_WIDTH), F32),
                        pltpu.VMEM((tt, GDN_WIDTH), F32), pltpu.VMEM((2, tt, LANES), F32)],
        compiler_params=_cparams(("parallel", "arbitrary")),
        name="gdn",
    )(qkv, z, ba, buf, s0, conv_w, ad, norm_w)


def _s5_prep_kernel(lam_ref, ldt_ref, bt_ref, cm_ref, ktoep_ref, bend_ref, cst_ref, lamc_ref):
    g_n, p_n, lc = S5_GROUPS, S5_STATE, S5_CHUNK
    lr, li = lam_ref[0], lam_ref[1]
    dt = jnp.exp(ldt_ref[...])
    mag = jnp.exp(lr * dt)
    ang = li * dt
    br, bi = mag * jnp.cos(ang), mag * jnp.sin(ang)
    den = lr * lr + li * li
    qr = ((br - 1.0) * lr + bi * li) / den
    qi = (bi * lr - (br - 1.0) * li) / den
    bbr = qr * bt_ref[0] - qi * bt_ref[1]
    bbi = qr * bt_ref[1] + qi * bt_ref[0]
    cr, cim = cm_ref[0], cm_ref[1]

    pw = [(jnp.ones_like(br), jnp.zeros_like(br))]
    for _ in range(lc):
        ar, ai = pw[-1]
        pw.append((ar * br - ai * bi, ar * bi + ai * br))
    lamc_ref[:, :, 0:p_n] = pw[lc][0]
    lamc_ref[:, :, p_n:2 * p_n] = pw[lc][1]

    for s in range(lc):
        ar, ai = pw[lc - 1 - s]
        bend_ref[:, s, :, 0:p_n] = ar * bbr - ai * bbi
        bend_ref[:, s, :, p_n:2 * p_n] = ar * bbi + ai * bbr
        ar, ai = pw[s + 1]
        cst_ref[:, s, :, 0:p_n] = ar * cr - ai * cim
        cst_ref[:, s, :, p_n:2 * p_n] = -(ar * cim + ai * cr)

    lane = lax.broadcasted_iota(jnp.int32, (S5_GROUP, lc * S5_GROUP), 1)
    for g in range(g_n):
        cpr = jnp.concatenate([pw[j][0][g] * cr[g] - pw[j][1][g] * cim[g] for j in range(lc)], axis=0)
        cpi = jnp.concatenate([pw[j][0][g] * cim[g] + pw[j][1][g] * cr[g] for j in range(lc)], axis=0)
        bh_r, bl_r = _split2(bbr[g])
        bh_i, bl_i = _split2(bbi[g])
        ch_r, cl_r = _split2(cpr)
        ch_i, cl_i = _split2(cpi)
        krow = (_dot_nt(bh_r, ch_r) + (_dot_nt(bh_r, cl_r) + _dot_nt(bl_r, ch_r))
                - (_dot_nt(bh_i, ch_i) + (_dot_nt(bh_i, cl_i) + _dot_nt(bl_i, ch_i))))
        for s in range(lc):
            sh = s * S5_GROUP
            blk = krow if s == 0 else jnp.where(lane >= sh, pltpu.roll(krow, sh, 1), 0.0)
            ktoep_ref[g, s * S5_GROUP:(s + 1) * S5_GROUP, :] = blk.astype(BF16)


def _s5_prep(lam, ldt, bt, cm):
    g_n, p_n, lc = S5_GROUPS, S5_STATE, S5_CHUNK
    w = lc * S5_GROUP
    return pl.pallas_call(
        _s5_prep_kernel,
        out_shape=[jax.ShapeDtypeStruct((g_n, w, w), BF16),
                   jax.ShapeDtypeStruct((g_n, lc, S5_GROUP, 2 * p_n), F32),
                   jax.ShapeDtypeStruct((g_n, lc, S5_GROUP, 2 * p_n), F32),
                   jax.ShapeDtypeStruct((g_n, 1, 2 * p_n), F32)],
        compiler_params=pltpu.CompilerParams(vmem_limit_bytes=VMEM_LIMIT),
        name="s5_prep",
    )(lam, ldt, bt, cm)


def _s5_kernel(u_ref, h0_ref, ktoep_ref, bend_ref, cst_ref, lamc_ref, d_ref, y_ref, hn_ref, *, n, nb):
    p_n = S5_STATE
    r_n = nb * n
    lane = lax.broadcasted_iota(jnp.int32, (1, 2 * p_n), 1)
    sign = jnp.where(lane < p_n, -1.0, 1.0)
    rowm = lax.broadcasted_iota(jnp.int32, (r_n, 2 * p_n), 0) % n
    if nb > 1:
        e_r = lax.broadcasted_iota(jnp.int32, (r_n, nb), 0)
        e_b = lax.broadcasted_iota(jnp.int32, (r_n, nb), 1)
        expand = (e_r // n == e_b).astype(BF16)
        s_b = lax.broadcasted_iota(jnp.int32, (nb, r_n), 0)
        s_r = lax.broadcasted_iota(jnp.int32, (nb, r_n), 1)
        pick_last = (s_r == s_b * n + (n - 1)).astype(BF16)

    def lam_tiles(g):
        lam = lamc_ref[pl.ds(g, 1), :]
        lam_sw = pltpu.roll(lam, p_n, 1)
        return (jnp.where(lane < p_n, lam, lam_sw),
                jnp.where(lane < p_n, lam_sw, lam) * sign)

    def cmul(x, a_rr, a_is):
        return x * a_rr + pltpu.roll(x, p_n, 1) * a_is

    def group_body(g, carry):
        u = u_ref[0, g]
        a_rr, a_is = lam_tiles(g)
        x_loc = _dot_x3(u, bend_ref[g])
        h0 = h0_ref[0, g]
        h0_rows = _dot_exact_lhs(expand, h0) if nb > 1 else h0
        xs = jnp.where(rowm == 0, h0_rows, pltpu.roll(x_loc, 1, 0))
        step = 1
        while step < n:
            sh = jnp.where(rowm >= step, pltpu.roll(xs, step, 0), 0.0)
            xs = xs + cmul(sh, a_rr, a_is)
            a_rr, a_is = a_rr * a_rr - a_is * a_is, 2.0 * a_rr * a_is
            step *= 2
        y = _dot(u.astype(BF16), ktoep_ref[g]) + _dot_nt(xs.astype(BF16), cst_ref[g].astype(BF16))
        y = y + d_ref[pl.ds(g, 1), :] * u
        y_ref[0, g] = jax.nn.gelu(y)
        b_rr, b_is = lam_tiles(g)
        x_end = cmul(xs, b_rr, b_is) + x_loc
        hn_ref[0, g] = _dot_exact_lhs(pick_last, x_end) if nb > 1 else x_end[n - 1:n, :]
        return carry

    lax.fori_loop(0, S5_GROUPS, group_body, 0)


def _s5(u, h0, ktoep, bend, cst, lamc, d_t, n):
    nblk, g_n, r_n, w = u.shape
    nb = r_n // n
    p2 = 2 * S5_STATE
    kern = functools.partial(_s5_kernel, n=n, nb=nb)
    return pl.pallas_call(
        kern,
        grid=(nblk,),
        in_specs=[pl.BlockSpec((1, g_n, r_n, w), lambda i: (i, 0, 0, 0)),
                  pl.BlockSpec((1, g_n, nb, p2), lambda i: (i, 0, 0, 0)),
                  _const_spec(ktoep.shape), _const_spec(bend.shape), _const_spec(cst.shape),
                  _const_spec(lamc.shape), _const_spec(d_t.shape)],
        out_specs=[pl.BlockSpec((1, g_n, r_n, w), lambda i: (i, 0, 0, 0)),
                   pl.BlockSpec((1, g_n, nb, p2), lambda i: (i, 0, 0, 0))],
        out_shape=[jax.ShapeDtypeStruct((nblk, g_n, r_n, w), F32),
                   jax.ShapeDtypeStruct((nblk, g_n, nb, p2), F32)],
        compiler_params=_cparams(("parallel",)),
        name="s5",
    )(u, h0, ktoep, bend, cst, lamc, d_t)


def _cc_kernel(cc_ref, buf_ref, w_ref, p_ref, yc_ref, nb_ref, xbuf, *, tt):
    ti = pl.program_id(1)
    hist = CC_KERNEL - 1
    pad = 32

    @pl.when(ti == 0)
    def _():
        xbuf[pad - hist:pad, :] = buf_ref[0]

    @pl.when(ti > 0)
    def _():
        xbuf[pad - hist:pad, :] = xbuf[pad + tt - hist:pad + tt, :]

    cc = cc_ref[0]
    xbuf[pad:pad + tt, :] = cc[:, 0:CC_WIDTH] * jax.nn.sigmoid(cc[:, CC_WIDTH:])
    nb_ref[0] = xbuf[pad + tt - hist:pad + tt, :]

    acc = w_ref[0:1, :] * xbuf[pad - hist:pad - hist + tt, :]
    for j in range(1, CC_KERNEL):
        acc = acc + w_ref[j:j + 1, :] * xbuf[pad - hist + j:pad - hist + j + tt, :]
    acc = acc + p_ref[0:1, :]
    mu = jnp.mean(acc, axis=-1, keepdims=True)
    xc = acc - mu
    var = jnp.mean(xc * xc, axis=-1, keepdims=True)
    yc_ref[0] = _silu(xc * lax.rsqrt(var + 1e-5) * p_ref[1:2, :] + p_ref[2:3, :])


def _cc(cc, buf, dw_w, params, tt):
    b, t, _ = cc.shape
    kern = functools.partial(_cc_kernel, tt=tt)
    return pl.pallas_call(
        kern,
        grid=(b, t // tt),
        in_specs=[pl.BlockSpec((1, tt, 2 * CC_WIDTH), lambda i, j: (i, j, 0)),
                  pl.BlockSpec((1, CC_KERNEL - 1, CC_WIDTH), lambda i, j: (i, 0, 0)),
                  _const_spec(dw_w.shape), _const_spec(params.shape)],
        out_specs=[pl.BlockSpec((1, tt, CC_WIDTH), lambda i, j: (i, j, 0)),
                   pl.BlockSpec((1, CC_KERNEL - 1, CC_WIDTH), lambda i, j: (i, 0, 0))],
        out_shape=[jax.ShapeDtypeStruct((b, t, CC_WIDTH), F32),
                   jax.ShapeDtypeStruct((b, CC_KERNEL - 1, CC_WIDTH), F32)],
        scratch_shapes=[pltpu.VMEM((tt + 32, CC_WIDTH), F32)],
        compiler_params=_cparams(("parallel", "arbitrary")),
        name="cc",
    )(cc, buf, dw_w, params)


def _out_ffn_kernel(h_ref, ya_ref, yb_ref, yc_ref, p_ref, gw_ref, gb_ref, woa_ref, wob_ref, woc_ref,
                    nf_ref, w1_ref, w3_ref, w2_ref, pew_ref, peg_ref, nfin_ref, o_ref, *, final):
    yb = yb_ref[...]
    yb = yb * jax.nn.sigmoid(_dot(yb.astype(BF16), gw_ref[...]) + gb_ref[...])
    h = h_ref[...] + (_dot(ya_ref[...].astype(BF16), woa_ref[...])
                      + _dot(yb.astype(BF16), wob_ref[...])
                      + _dot(yc_ref[...].astype(BF16), woc_ref[...]))
    ms = jnp.mean(h * h, axis=-1, keepdims=True)
    hf = ((h * lax.rsqrt(ms + 1e-6)) * nf_ref[...]).astype(BF16)

    def ffn_body(j, acc):
        a = _dot(hf, w1_ref[j])
        b = _dot(hf, w3_ref[j])
        return acc + _dot((_silu(a) * b).astype(BF16), w2_ref[j])

    h = h + lax.fori_loop(0, FFN_HIDDEN // FFN_CHUNK, ffn_body, jnp.zeros_like(h))
    pe = _dot(p_ref[...].astype(BF16), pew_ref[...])
    h = h + pe * jax.nn.sigmoid(_dot(h.astype(BF16), peg_ref[...]))
    if final:
        ms = jnp.mean(h * h, axis=-1, keepdims=True)
        h = (h * lax.rsqrt(ms + 1e-6)) * nfin_ref[...]
    o_ref[...] = h


def _out_ffn(h, ya, yb, yc, p, gw, gb, woa, wob, woc, nf, w1, w3, w2, pew, peg, nfin, tm, final):
    n = h.shape[0]
    row = lambda w: pl.BlockSpec((tm, w), lambda i: (i, 0))
    consts = [gw, gb, woa, wob, woc, nf, w1, w3, w2, pew, peg, nfin]
    return pl.pallas_call(
        functools.partial(_out_ffn_kernel, final=final),
        grid=(n // tm,),
        in_specs=[row(D_MODEL), row(GDN_WIDTH), row(S5_WIDTH), row(CC_WIDTH), row(PLE_DIM)]
        + [_const_spec(a.shape) for a in consts],
        out_specs=row(D_MODEL),
        out_shape=jax.ShapeDtypeStruct((n, D_MODEL), F32),
        compiler_params=_cparams(("parallel",)),
        name="out_ffn",
    )(h, ya, yb, yc, p, *consts)


def _tile_rows(n, want):
    t = min(n, want)
    while n % t:
        t //= 2
    return t


def _prep_layer(i, W):
    w_in = W['w_in'][i]
    L = {}
    L['g_mix'] = W['norm_mix'][i][None, :]
    L['wq'] = w_in[:, :OFF_Z].astype(BF16)
    L['wz'] = w_in[:, OFF_Z:OFF_BA].astype(BF16)
    L['wba'] = jnp.pad(w_in[:, OFF_BA:OFF_S5], ((0, 0), (0, LANES - 2 * GDN_HEADS))).astype(BF16)
    L['wu'] = w_in[:, OFF_S5:OFF_CC].astype(BF16)
    L['wc'] = w_in[:, OFF_CC:].astype(BF16)
    L['conv_w'] = W['gdn_conv_w'][i]
    padl = (GDN_HEADS, LANES - 2 * GDN_HEADS)
    L['ad'] = jnp.stack([jnp.pad(W['gdn_a_log'][i], padl), jnp.pad(W['gdn_dt_bias'][i], padl)])
    L['gdn_norm'] = W['gdn_norm'][i][None, :]
    lam = jnp.stack([W['s5_lam_re'][i], W['s5_lam_im'][i]])[:, :, None, :]
    bt =jnp.stack([jnp.swapaxes(W['s5_b_re'][i], 1, 2), jnp.swapaxes(W['s5_b_im'][i], 1, 2)])
    cm = jnp.stack([W['s5_c_re'][i], W['s5_c_im'][i]])
    ktoep, bend, cst, lamc = _s5_prep(lam, W['s5_log_dt'][i][:, None, None], bt, cm)
    w = S5_CHUNK * S5_GROUP
    L['ktoep'] = ktoep
    L['bend'] = bend.reshape(S5_GROUPS, w, 2 * S5_STATE)
    L['cst'] = cst.reshape(S5_GROUPS, w, 2 * S5_STATE)
    L['lamc'] = lamc.reshape(S5_GROUPS, 2 * S5_STATE)
    L['d_t'] = jnp.tile(W['s5_d'][i].reshape(S5_GROUPS, S5_GROUP), (1, S5_CHUNK))
    L['glu_w'] = W['s5_glu_w'][i].astype(BF16)
    L['glu_b'] = W['s5_glu_b'][i][None, :]
    L['dw_w'] = W['cc_dw_w'][i]
    L['cc_p'] = jnp.stack([W['cc_dw_b'][i], W['cc_ln_g'][i], W['cc_ln_b'][i]])
    wo = W['w_out'][i].astype(BF16)
    L['woa'], L['wob'], L['woc'] = wo[:GDN_WIDTH], wo[GDN_WIDTH:GDN_WIDTH + S5_WIDTH], wo[GDN_WIDTH + S5_WIDTH:]
    L['g_ffn'] = W['norm_ffn'][i][None, :]
    nck = FFN_HIDDEN // FFN_CHUNK
    L['w1'] = jnp.swapaxes(W['ffn_w1'][i].astype(BF16).reshape(D_MODEL, nck, FFN_CHUNK), 0, 1)
    L['w3'] = jnp.swapaxes(W['ffn_w3'][i].astype(BF16).reshape(D_MODEL, nck, FFN_CHUNK), 0, 1)
    L['w2'] = W['ffn_w2'][i].astype(BF16).reshape(nck, FFN_CHUNK, D_MODEL)
    L['pew'] = W['pe_w'][i].astype(BF16)
    L['peg'] = W['pe_gate_w'][i].astype(BF16)
    L['g_fin'] = W['norm_final'][None, :]
    return L


def _layer(h, p, st_gdn, st_gconv, st_s5, st_conv, L, final):
    b, t, _ = h.shape
    n = b * t
    tm = _tile_rows(n, 512)
    qkv, z, ba, u5, cc = _in_proj(h.reshape(n, D_MODEL), L['g_mix'], L['wq'], L['wz'], L['wba'], L['wu'], L['wc'], tm)

    chunk = 64 if t % 64 == 0 else t
    tt = _tile_rows(t, 256)
    ya, s_gdn, b_gconv = _gdn(qkv.reshape(b, t, QKV_COLS), z.reshape(b, t, GDN_WIDTH), ba.reshape(b, t, LANES),
                              st_gconv, st_gdn, L['conv_w'], L['ad'], L['gdn_norm'], tt, chunk)

    nck = t // S5_CHUNK
    nb = _tile_rows(b, max(1, S5_ROWS // nck))
    u_f = jnp.transpose(u5.reshape(b // nb, nb, nck, S5_CHUNK, S5_GROUPS, S5_GROUP), (0, 4, 1, 2, 3, 5))
    u_f = u_f.reshape(b // nb, S5_GROUPS, nb * nck, S5_CHUNK * S5_GROUP)
    h0 = jnp.concatenate([st_s5[..., 0], st_s5[..., 1]], axis=-1)
    h0 = jnp.swapaxes(h0.reshape(b // nb, nb, S5_GROUPS, 2 * S5_STATE), 1, 2)
    y_f, hn = _s5(u_f, h0, L['ktoep'], L['bend'], L['cst'], L['lamc'], L['d_t'], nck)
    yb = jnp.transpose(y_f.reshape(b // nb, S5_GROUPS, nb, nck, S5_CHUNK, S5_GROUP), (0, 2, 3, 4, 1, 5))
    yb = yb.reshape(n, S5_WIDTH)
    hn = jnp.swapaxes(hn, 1, 2).reshape(b, S5_GROUPS, 2 * S5_STATE)
    s_s5 = jnp.stack([hn[..., :S5_STATE], hn[..., S5_STATE:]], axis=-1)

    yc, b_conv = _cc(cc.reshape(b, t, 2 * CC_WIDTH), st_conv, L['dw_w'], L['cc_p'], _tile_rows(t, 512))

    h_new = _out_ffn(h.reshape(n, D_MODEL), ya.reshape(n, GDN_WIDTH), yb, yc.reshape(n, CC_WIDTH),
                     p.reshape(n, PLE_DIM), L['glu_w'], L['glu_b'], L['woa'], L['wob'], L['woc'], L['g_ffn'],
                     L['w1'], L['w3'], L['w2'], L['pew'], L['peg'], L['g_fin'], tm, final)
    return h_new.reshape(b, t, D_MODEL), s_gdn, b_gconv, s_s5, b_conv


def _trunk(x, p, st_gdn, st_gconv, st_s5, st_conv, layers):
    h = x
    outs = ([], [], [], [])
    depth = len(layers)
    for i, L in enumerate(layers):
        h, *st = _layer(h, p[i], st_gdn[i], st_gconv[i], st_s5[i], st_conv[i], L, i == depth - 1)
        for acc, s in zip(outs, st):
            acc.append(s)
    return (h,) + tuple(jnp.stack(o) for o in outs)


def kernel(x_prompt, x_sample, p_prompt, p_sample, state_gdn, state_gdn_conv, state_s5, state_conv, norm_mix, w_in, gdn_conv_w, gdn_a_log, gdn_dt_bias, gdn_norm, s5_lam_re, s5_lam_im, s5_log_dt, s5_b_re, s5_b_im, s5_c_re, s5_c_im, s5_d, s5_glu_w, s5_glu_b, cc_dw_w, cc_dw_b, cc_ln_g, cc_ln_b, w_out, norm_ffn, ffn_w1, ffn_w3, ffn_w2, pe_w, pe_gate_w, norm_final):
    W = dict(norm_mix=norm_mix, w_in=w_in, gdn_conv_w=gdn_conv_w, gdn_a_log=gdn_a_log, gdn_dt_bias=gdn_dt_bias,
             gdn_norm=gdn_norm, s5_lam_re=s5_lam_re, s5_lam_im=s5_lam_im, s5_log_dt=s5_log_dt, s5_b_re=s5_b_re,
             s5_b_im=s5_b_im, s5_c_re=s5_c_re, s5_c_im=s5_c_im, s5_d=s5_d, s5_glu_w=s5_glu_w, s5_glu_b=s5_glu_b,
             cc_dw_w=cc_dw_w, cc_dw_b=cc_dw_b, cc_ln_g=cc_ln_g, cc_ln_b=cc_ln_b, w_out=w_out, norm_ffn=norm_ffn,
             ffn_w1=ffn_w1, ffn_w3=ffn_w3, ffn_w2=ffn_w2, pe_w=pe_w, pe_gate_w=pe_gate_w, norm_final=norm_final)
    depth = w_in.shape[0]
    layers = [_prep_layer(i, W) for i in range(depth)]
    bp = x_prompt.shape[0]
    z_gdn = jnp.zeros((depth, bp, GDN_HEADS, GDN_HEAD_DIM, GDN_HEAD_DIM), F32)
    z_gconv = jnp.zeros((depth, bp, GDN_CONV - 1, QKV_COLS), F32)
    z_s5 = jnp.zeros((depth, bp, S5_GROUPS, S5_STATE, 2), F32)
    z_conv = jnp.zeros((depth, bp, CC_KERNEL - 1, CC_WIDTH), F32)
    y_p, gdn_p, gconv_p, s5_p, conv_p = _trunk(x_prompt, p_prompt, z_gdn, z_gconv, z_s5, z_conv, layers)
    y_s, gdn_s, gconv_s, s5_s, conv_s = _trunk(x_sample, p_sample, state_gdn, state_gdn_conv, state_s5,
                                               state_conv, layers)
    return (y_p, y_s, gdn_p, gconv_p, s5_p, conv_p, gdn_s, gconv_s, s5_s, conv_s)
```

```python
import functools
import math

import jax
import jax.numpy as jnp
from jax import lax
from jax.experimental import pallas as pl
from jax.experimental.pallas import tpu as pltpu

F32 = jnp.float32
BF16 = jnp.bfloat16

D_MODEL = 1024
GDN_HEADS = 4
GDN_HEAD_DIM = 128
GDN_WIDTH = GDN_HEADS * GDN_HEAD_DIM
GDN_CONV = 4
QKV_COLS = 3 * GDN_WIDTH
S5_WIDTH = 256
S5_GROUP = 16
S5_GROUPS = 16
S5_STATE = 64
S5_CHUNK = 16
CC_WIDTH = 256
CC_KERNEL = 31
FFN_HIDDEN = 2816
ROW_TILE = 512
PLE_DIM = 256
OFF_Z = QKV_COLS
OFF_BA = OFF_Z + GDN_WIDTH
OFF_S5 = OFF_BA + 2 * GDN_HEADS
OFF_CC = OFF_S5 + S5_WIDTH
IN_COLS = OFF_CC + 2 * CC_WIDTH
LANES = 128
VMEM_LIMIT = 56 * 1024 * 1024


def _cparams(sem):
    return pltpu.CompilerParams(dimension_semantics=sem, vmem_limit_bytes=VMEM_LIMIT)


def _const_spec(shape):
    nd = len(shape)
    return pl.BlockSpec(shape, lambda *_: (0,) * nd, pipeline_mode=pl.Buffered(1))


def _split2(x):
    hi = x.astype(BF16)
    lo = (x - hi.astype(F32)).astype(BF16)
    return hi, lo


def _split3(x):
    hi = x.astype(BF16)
    r = x - hi.astype(F32)
    mid = r.astype(BF16)
    lo = (r - mid.astype(F32)).astype(BF16)
    return hi, mid, lo


def _dot(a, b):
    return jnp.dot(a, b, preferred_element_type=F32)


def _dot_nt(a, b):
    return lax.dot_general(a, b, (((1,), (1,)), ((), ())), preferred_element_type=F32)


def _dot_tn(a, b):
    return lax.dot_general(a, b, (((0,), (0,)), ((), ())), preferred_element_type=F32)


def _dot_x3(a, b):
    ah, al = _split2(a)
    bh, bl = _split2(b)
    return _dot(ah, bh) + (_dot(ah, bl) + _dot(al, bh))


def _dot_exact_lhs(l_bf16, x):
    h, m, lo = _split3(x)
    return _dot(l_bf16, h) + (_dot(l_bf16, m) + _dot(l_bf16, lo))


def _silu(x):
    return x * jax.nn.sigmoid(x)


def _seq_tiling(b, t, tm):
    if t >= tm:
        assert t % tm == 0
        return 1, t // tm
    assert tm % t == 0 and (b * t) % tm == 0
    return tm // t, 1


def _s5_block_spec(tm, tps):
    return pl.BlockSpec((1, S5_GROUPS, tm // S5_CHUNK, S5_CHUNK * S5_GROUP), lambda i: (i // tps, 0, i % tps, 0))


def _inproj_kernel(h_ref, g_ref, wq_ref, wz_ref, wba_ref, wu_ref, wc_ref,
                   qkv_ref, z_ref, ba_ref, uf_ref, cc_ref, u_scr, *, tm):
    x = h_ref[...]
    ms = jnp.mean(x * x, axis=-1, keepdims=True)
    hn = (x * lax.rsqrt(ms + 1e-6)) * g_ref[...]
    hb = hn.astype(BF16)
    qkv_ref[...] = _dot(hb, wq_ref[...])
    z_ref[...] = _dot(hb, wz_ref[...])
    ba_ref[...] = _dot(hb, wba_ref[...])
    cc_ref[...] = _dot(hb, wc_ref[...])
    u = _dot(hb, wu_ref[...])
    nck = tm // S5_CHUNK
    gpl = LANES // S5_GROUP
    for half in range(S5_WIDTH // LANES):
        u_scr[half] = u[:, half * LANES:(half + 1) * LANES]
    for s in range(S5_CHUNK):
        for half in range(S5_WIDTH // LANES):
            rows = u_scr[half, pl.ds(s, nck, stride=S5_CHUNK), :]
            for g in range(gpl):
                uf_ref[0, half * gpl + g, :, s * S5_GROUP:(s + 1) * S5_GROUP] = (
                    rows[:, g * S5_GROUP:(g + 1) * S5_GROUP])


def _in_proj(h, g, wq, wz, wba, wu, wc, tm, tps):
    n = h.shape[0]
    row = lambda w: pl.BlockSpec((tm, w), lambda i: (i, 0))
    nblk = n // (tm * tps)
    return pl.pallas_call(
        functools.partial(_inproj_kernel, tm=tm),
        grid=(n // tm,),
        in_specs=[row(D_MODEL), _const_spec((1, D_MODEL)), _const_spec(wq.shape), _const_spec(wz.shape),
                  _const_spec(wba.shape), _const_spec(wu.shape), _const_spec(wc.shape)],
        out_specs=[row(QKV_COLS), row(GDN_WIDTH), row(LANES), _s5_block_spec(tm, tps), row(2 * CC_WIDTH)],
        out_shape=[jax.ShapeDtypeStruct((n, QKV_COLS), F32), jax.ShapeDtypeStruct((n, GDN_WIDTH), F32),
                   jax.ShapeDtypeStruct((n, LANES), F32),
                   jax.ShapeDtypeStruct((nblk, S5_GROUPS, tps * tm // S5_CHUNK, S5_CHUNK * S5_GROUP), F32),
                   jax.ShapeDtypeStruct((n, 2 * CC_WIDTH), F32)],
        scratch_shapes=[pltpu.VMEM((S5_WIDTH // LANES, tm, LANES), F32)],
        compiler_params=_cparams(("parallel",)),
        name="in_proj",
    )(h, g, wq, wz, wba, wu, wc)


def _gdn_kernel(qkv_ref, z_ref, ba_ref, buf_ref, s0_ref, cw_ref, ad_ref, nw_ref,
                ya_ref, sn_ref, nb_ref,
                xbuf, s_scr, q_scr, k_scr, v_scr, gb_scr, *, tt, chunk):
    ti = pl.program_id(1)
    nt = pl.num_programs(1)
    hd = GDN_HEAD_DIM
    nh = GDN_HEADS
    c = chunk
    pad = 8
    hist = GDN_CONV - 1

    @pl.when(ti == 0)
    def _():
        xbuf[pad - hist:pad, :] = buf_ref[0]
        s_scr[...] = s0_ref[0]

    @pl.when(ti > 0)
    def _():
        xbuf[pad - hist:pad, :] = xbuf[pad + tt - hist:pad + tt, :]

    xbuf[pad:pad + tt, :] = qkv_ref[0]
    nb_ref[0] = xbuf[pad + tt - hist:pad + tt, :]

    for cb in range(3 * nh):
        cols = slice(cb * hd, (cb + 1) * hd)
        acc = cw_ref[0:1, cols] * xbuf[pad - hist:pad - hist + tt, cols]
        for j in range(1, GDN_CONV):
            acc = acc + cw_ref[j:j + 1, cols] * xbuf[pad - hist + j:pad - hist + j + tt, cols]
        y = _silu(acc)
        h = cb % nh
        if cb < nh:
            q_scr[:, h * hd:(h + 1) * hd] = y * (lax.rsqrt(jnp.sum(y * y, axis=-1, keepdims=True) + 1e-6)
                                                 * (hd ** -0.5))
        elif cb < 2 * nh:
            k_scr[:, h * hd:(h + 1) * hd] = y * lax.rsqrt(jnp.sum(y * y, axis=-1, keepdims=True) + 1e-6)
        else:
            v_scr[:, h * hd:(h + 1) * hd] = y

    ba = ba_ref[0]
    beta = jax.nn.sigmoid(ba)
    sp_in = ba + ad_ref[1:2, :]
    softplus = jnp.maximum(sp_in, 0.0) + jnp.log(1.0 + jnp.exp(-jnp.abs(sp_in)))
    g = -jnp.exp(ad_ref[0:1, :]) * softplus
    gb_scr[0] = beta
    gb_scr[1] = g

    nck = tt // c
    t_r = lax.broadcasted_iota(jnp.int32, (tt, tt), 0)
    t_c = lax.broadcasted_iota(jnp.int32, (tt, tt), 1)
    tril_bd = ((t_r // c == t_c // c) & (t_r >= t_c)).astype(BF16)
    pi = lax.broadcasted_iota(jnp.int32, (tt, nh * c), 0) % c
    pc = lax.broadcasted_iota(jnp.int32, (tt, nh * c), 1)
    ps = pc % c
    p_head = pc // c
    ci_r = lax.broadcasted_iota(jnp.int32, (c, nh * c), 0)
    ci_s = lax.broadcasted_iota(jnp.int32, (c, nh * c), 1) % c
    strict4 = ci_r > ci_s
    causal4 = ci_r >= ci_s
    eye4 = (ci_r == ci_s).astype(F32)
    bd_r = lax.broadcasted_iota(jnp.int32, (nh * c, nh * c), 0) // c
    bd_c = lax.broadcasted_iota(jnp.int32, (nh * c, nh * c), 1) // c
    bd_mask = (bd_r == bd_c).astype(BF16)
    kbd_r = lax.broadcasted_iota(jnp.int32, (nh * c, GDN_WIDTH), 0) // c
    kbd_c = lax.broadcasted_iota(jnp.int32, (nh * c, GDN_WIDTH), 1) // hd
    kbd_mask = (kbd_r == kbd_c).astype(BF16)
    n_dbl = int(math.log2(c)) - 1

    def blockdiag(xb):
        return jnp.concatenate([xb] * nh, axis=0) * bd_mask

    def mm_packed(a4, b4):
        return _dot(a4.astype(BF16), blockdiag(b4.astype(BF16)))

    def pad_rows(x, h):
        z = jnp.zeros_like(x)
        return jnp.concatenate([x if j == h else z for j in range(nh)], axis=0)

    g_all = gb_scr[1]
    gc_all = _dot_exact_lhs(tril_bd, g_all)
    g4 = jnp.zeros((tt, nh * c), F32)
    for h in range(nh):
        g4 = jnp.where(p_head == h, jnp.broadcast_to(g_all[:, nh + h:nh + h + 1], (tt, nh * c)), g4)
    decay_all = jnp.exp(_dot_exact_lhs(tril_bd, jnp.where(pi > ps, g4, 0.0)))

    chunk_rows = [slice(ck * c, (ck + 1) * c) for ck in range(nck)]
    g_last_b = jnp.concatenate(
        [jnp.broadcast_to(gc_all[(ck + 1) * c - 1:(ck + 1) * c], (c, LANES)) for ck in range(nck)], axis=0)
    e_gc = jnp.exp(gc_all)
    e_rem = jnp.exp(g_last_b - gc_all)
    e_last = [jnp.exp(gc_all[(ck + 1) * c - 1:(ck + 1) * c]) for ck in range(nck)]
    beta_all = gb_scr[0]
    q_all = q_scr[...]
    k_all = k_scr[...]
    v_all = v_scr[...]
    kb_parts, wq_b, kd_b, rhs_b = [], [], [], []
    for h in range(nh):
        sl = slice(h * hd, (h + 1) * hd)
        b_h = jnp.broadcast_to(beta_all[:, h:h + 1], (tt, hd))
        eg_h = jnp.broadcast_to(e_gc[:, nh + h:nh + h + 1], (tt, hd))
        er_h = jnp.broadcast_to(e_rem[:, nh + h:nh + h + 1], (tt, hd))
        kb_h = k_all[:, sl] * b_h
        kb_parts.append(kb_h)
        wq_b.append((q_all[:, sl] * eg_h).astype(BF16))
        kd_b.append((k_all[:, sl] * er_h).astype(BF16))
        rhs_b.append(jnp.concatenate([v_all[:, sl] * b_h, kb_h * eg_h], axis=1).astype(BF16))
    kb_b = jnp.concatenate(kb_parts, axis=1).astype(BF16)
    q_b = q_all.astype(BF16)
    k_b = k_all.astype(BF16)

    kq = []
    for rows in chunk_rows:
        k_bd = jnp.concatenate([k_b[rows]] * nh, axis=0) * kbd_mask
        kq.append(_dot_nt(jnp.concatenate([kb_b[rows], q_b[rows]], axis=0), k_bd))
    x4, t4, aqk4 = [], [], []
    for ck, rows in enumerate(chunk_rows):
        n4 = jnp.where(strict4, kq[ck][0:c] * decay_all[rows], 0.0)
        aqk4.append(jnp.where(causal4, kq[ck][c:2 * c] * decay_all[rows], 0.0).astype(BF16))
        x4.append(n4)
        t4.append(eye4 - n4)
    for _ in range(n_dbl):
        x4 = [mm_packed(x, x) for x in x4]
        t4 = [t + mm_packed(t, x) for t, x in zip(t4, x4)]
    uw = []
    for ck, rows in enumerate(chunk_rows):
        t4b = t4[ck].astype(BF16)
        uw.append([_dot(t4b, pad_rows(rhs_b[h][rows], h)) for h in range(nh)])

    s_cur = [s_scr[h] for h in range(nh)]
    for ck, rows in enumerate(chunk_rows):
        ws = [_dot(jnp.concatenate([uw[ck][h][:, hd:2 * hd].astype(BF16), wq_b[h][rows]], axis=0),
                   s_cur[h].astype(BF16)) for h in range(nh)]
        v_new = [(uw[ck][h][:, 0:hd] - ws[h][0:c]).astype(BF16) for h in range(nh)]
        s_add = [_dot_tn(kd_b[h][rows], v_new[h]) for h in range(nh)]
        o_att = [_dot(aqk4[ck], pad_rows(v_new[h], h)) for h in range(nh)]
        for h in range(nh):
            sl = slice(h * hd, (h + 1) * hd)
            dl = jnp.broadcast_to(e_last[ck][:, nh + h:nh + h + 1], (hd, hd))
            s_cur[h] = s_cur[h] * dl + s_add[h]
            o_h = ws[h][c:2 * c] + o_att[h]
            o_n = o_h * lax.rsqrt(jnp.mean(o_h * o_h, axis=-1, keepdims=True) + 1e-6) * nw_ref[...]
            ya_ref[0, rows, sl] = o_n * _silu(z_ref[0, rows, sl])
    for h in range(nh):
        s_scr[h] = s_cur[h]

    @pl.when(ti == nt - 1)
    def _():
        sn_ref[0] = s_scr[...]


def _gdn(qkv, z, ba, buf, s0, conv_w, ad, norm_w, tt, chunk):
    b, t, _ = qkv.shape
    kern = functools.partial(_gdn_kernel, tt=tt, chunk=chunk)
    tile = lambda w: pl.BlockSpec((1, tt, w), lambda i, j: (i, j, 0))
    per_b3 = lambda s: pl.BlockSpec((1,) + s, lambda i, j: (i,) + (0,) * len(s))
    return pl.pallas_call(
        kern,
        grid=(b, t // tt),
        in_specs=[tile(QKV_COLS), tile(GDN_WIDTH), tile(LANES), per_b3((GDN_CONV - 1, QKV_COLS)),
                  per_b3((GDN_HEADS, GDN_HEAD_DIM, GDN_HEAD_DIM)), _const_spec(conv_w.shape),
                  _const_spec(ad.shape), _const_spec(norm_w.shape)],
        out_specs=[tile(GDN_WIDTH), per_b3((GDN_HEADS, GDN_HEAD_DIM, GDN_HEAD_DIM)),
                   per_b3((GDN_CONV - 1, QKV_COLS))],
        out_shape=[jax.ShapeDtypeStruct((b, t, GDN_WIDTH), F32),
                   jax.ShapeDtypeStruct((b, GDN_HEADS, GDN_HEAD_DIM, GDN_HEAD_DIM), F32),
                   jax.ShapeDtypeStruct((b, GDN_CONV - 1, QKV_COLS), F32)],
        scratch_shapes=[pltpu.VMEM((tt + 8, QKV_COLS), F32),
                        pltpu.VMEM((GDN_HEADS, GDN_HEAD_DIM, GDN_HEAD_DIM), F32),
                        pltpu.VMEM((tt, GDN_WIDTH), F32), pltpu.VMEM((tt, GDN_WIDTH), F32),
                        pltpu.VMEM((tt, GDN_WIDTH), F32), pltpu.VMEM((2, tt, LANES), F32)],
        compiler_params=_cparams(("parallel", "arbitrary")),
        name="gdn",
    )(qkv, z, ba, buf, s0, conv_w, ad, norm_w)


def _s5_prep_kernel(lam_ref, ldt_ref, bt_ref, cm_ref, ktoep_ref, bend_ref, cst_ref, lamc_ref):
    g_n, p_n, lc = S5_GROUPS, S5_STATE, S5_CHUNK
    lr, li = lam_ref[0], lam_ref[1]
    dt = jnp.exp(ldt_ref[...])
    mag = jnp.exp(lr * dt)
    ang = li * dt
    br, bi = mag * jnp.cos(ang), mag * jnp.sin(ang)
    den = lr * lr + li * li
    qr = ((br - 1.0) * lr + bi * li) / den
    qi = (bi * lr - (br - 1.0) * li) / den
    bbr = qr * bt_ref[0] - qi * bt_ref[1]
    bbi = qr * bt_ref[1] + qi * bt_ref[0]
    cr, cim = cm_ref[0], cm_ref[1]

    pw = [(jnp.ones_like(br), jnp.zeros_like(br))]
    for _ in range(lc):
        ar, ai = pw[-1]
        pw.append((ar * br - ai * bi, ar * bi + ai * br))
    lamc_ref[:, :, 0:p_n] = pw[lc][0]
    lamc_ref[:, :, p_n:2 * p_n] = pw[lc][1]

    for s in range(lc):
        ar, ai = pw[lc - 1 - s]
        bend_ref[:, s, :, 0:p_n] = ar * bbr - ai * bbi
        bend_ref[:, s, :, p_n:2 * p_n] = ar * bbi + ai * bbr
        ar, ai = pw[s + 1]
        cst_ref[:, s, :, 0:p_n] = ar * cr - ai * cim
        cst_ref[:, s, :, p_n:2 * p_n] = -(ar * cim + ai * cr)

    lane = lax.broadcasted_iota(jnp.int32, (S5_GROUP, lc * S5_GROUP), 1)
    for g in range(g_n):
        cpr = jnp.concatenate([pw[j][0][g] * cr[g] - pw[j][1][g] * cim[g] for j in range(lc)], axis=0)
        cpi = jnp.concatenate([pw[j][0][g] * cim[g] + pw[j][1][g] * cr[g] for j in range(lc)], axis=0)
        bh_r, bl_r = _split2(bbr[g])
        bh_i, bl_i = _split2(bbi[g])
        ch_r, cl_r = _split2(cpr)
        ch_i, cl_i = _split2(cpi)
        krow = (_dot_nt(bh_r, ch_r) + (_dot_nt(bh_r, cl_r) + _dot_nt(bl_r, ch_r))
                - (_dot_nt(bh_i, ch_i) + (_dot_nt(bh_i, cl_i) + _dot_nt(bl_i, ch_i))))
        for s in range(lc):
            sh = s * S5_GROUP
            blk = krow if s == 0 else jnp.where(lane >= sh, pltpu.roll(krow, sh, 1), 0.0)
            ktoep_ref[g, s * S5_GROUP:(s + 1) * S5_GROUP, :] = blk.astype(BF16)


def _s5_prep(lam, ldt, bt, cm):
    g_n, p_n, lc = S5_GROUPS, S5_STATE, S5_CHUNK
    w = lc * S5_GROUP
    return pl.pallas_call(
        _s5_prep_kernel,
        out_shape=[jax.ShapeDtypeStruct((g_n, w, w), BF16),
                   jax.ShapeDtypeStruct((g_n, lc, S5_GROUP, 2 * p_n), F32),
                   jax.ShapeDtypeStruct((g_n, lc, S5_GROUP, 2 * p_n), F32),
                   jax.ShapeDtypeStruct((g_n, 1, 2 * p_n), F32)],
        compiler_params=pltpu.CompilerParams(vmem_limit_bytes=VMEM_LIMIT),
        name="s5_prep",
    )(lam, ldt, bt, cm)


def _s5_kernel(u_ref, h0_ref, ktoep_ref, bend_ref, cst_ref, lamc_ref, d_ref, y_ref, hn_ref, *, n, nb):
    p_n = S5_STATE
    r_n = nb * n
    lane = lax.broadcasted_iota(jnp.int32, (1, 2 * p_n), 1)
    sign = jnp.where(lane < p_n, -1.0, 1.0)
    rowm = lax.broadcasted_iota(jnp.int32, (r_n, 2 * p_n), 0) % n
    if nb > 1:
        e_r = lax.broadcasted_iota(jnp.int32, (r_n, nb), 0)
        e_b = lax.broadcasted_iota(jnp.int32, (r_n, nb), 1)
        expand = (e_r // n == e_b).astype(BF16)
        s_b = lax.broadcasted_iota(jnp.int32, (nb, r_n), 0)
        s_r = lax.broadcasted_iota(jnp.int32, (nb, r_n), 1)
        pick_last = (s_r == s_b * n + (n - 1)).astype(BF16)

    def lam_tiles(g):
        lam = lamc_ref[pl.ds(g, 1), :]
        lam_sw = pltpu.roll(lam, p_n, 1)
        return (jnp.where(lane < p_n, lam, lam_sw),
                jnp.where(lane < p_n, lam_sw, lam) * sign)

    def cmul(x, a_rr, a_is):
        return x * a_rr + pltpu.roll(x, p_n, 1) * a_is

    def group_body(g, carry):
        u = u_ref[0, g]
        a_rr, a_is = lam_tiles(g)
        x_loc = _dot_x3(u, bend_ref[g])
        h0 = h0_ref[0, g]
        h0_rows = _dot_exact_lhs(expand, h0) if nb > 1 else h0
        xs = jnp.where(rowm == 0, h0_rows, pltpu.roll(x_loc, 1, 0))
        step = 1
        while step < n:
            sh = jnp.where(rowm >= step, pltpu.roll(xs, step, 0), 0.0)
            xs = xs + cmul(sh, a_rr, a_is)
            a_rr, a_is = a_rr * a_rr - a_is * a_is, 2.0 * a_rr * a_is
            step *= 2
        y = _dot(u.astype(BF16), ktoep_ref[g]) + _dot_nt(xs.astype(BF16), cst_ref[g].astype(BF16))
        y = y + d_ref[pl.ds(g, 1), :] * u
        y_ref[0, g] = jax.nn.gelu(y)
        b_rr, b_is = lam_tiles(g)
        x_end = cmul(xs, b_rr, b_is) + x_loc
        hn_ref[0, g] = _dot_exact_lhs(pick_last, x_end) if nb > 1 else x_end[n - 1:n, :]
        return carry

    lax.fori_loop(0, S5_GROUPS, group_body, 0)


def _s5(u, h0, ktoep, bend, cst, lamc, d_t, n):
    nblk, g_n, r_n, w = u.shape
    nb = r_n // n
    p2 = 2 * S5_STATE
    kern = functools.partial(_s5_kernel, n=n, nb=nb)
    return pl.pallas_call(
        kern,
        grid=(nblk,),
        in_specs=[pl.BlockSpec((1, g_n, r_n, w), lambda i: (i, 0, 0, 0)),
                  pl.BlockSpec((1, g_n, nb, p2), lambda i: (i, 0, 0, 0)),
                  _const_spec(ktoep.shape), _const_spec(bend.shape), _const_spec(cst.shape),
                  _const_spec(lamc.shape), _const_spec(d_t.shape)],
        out_specs=[pl.BlockSpec((1, g_n, r_n, w), lambda i: (i, 0, 0, 0)),
                   pl.BlockSpec((1, g_n, nb, p2), lambda i: (i, 0, 0, 0))],
        out_shape=[jax.ShapeDtypeStruct((nblk, g_n, r_n, w), F32),
                   jax.ShapeDtypeStruct((nblk, g_n, nb, p2), F32)],
        compiler_params=_cparams(("parallel",)),
        name="s5",
    )(u, h0, ktoep, bend, cst, lamc, d_t)


CC_HIST = CC_KERNEL - 1
CC_PAD = 32


def _cc_module(cc, buf_ref, w_ref, p_ref, nb_ref, xbuf, sh_scr, *, ns, rs, first_tile):
    x = cc[:, 0:CC_WIDTH] * jax.nn.sigmoid(cc[:, CC_WIDTH:])
    lo = CC_PAD - CC_HIST
    if ns == 1:
        @pl.when(first_tile)
        def _():
            xbuf[0, lo:CC_PAD, :] = buf_ref[0]

        @pl.when(jnp.logical_not(first_tile))
        def _():
            xbuf[0, lo:CC_PAD, :] = xbuf[0, rs + lo:rs + CC_PAD, :]
    outs = []
    for q in range(ns):
        if ns > 1:
            xbuf[q, lo:CC_PAD, :] = buf_ref[q]
        xbuf[q, CC_PAD:CC_PAD + rs, :] = x[q * rs:(q + 1) * rs]
        nb_ref[q] = xbuf[q, rs + lo:rs + CC_PAD, :]
        for a in range(1, 8):
            sh_scr[a - 1] = xbuf[q, a:a + rs + CC_PAD - 8, :]
        acc = None
        for j in range(CC_KERNEL):
            a, m = (lo + j) % 8, (lo + j) // 8
            win = sh_scr[a - 1, 8 * m:8 * m + rs, :] if a else xbuf[q, 8 * m:8 * m + rs, :]
            term = w_ref[j:j + 1, :] * win
            acc = term if acc is None else acc + term
        outs.append(acc)
    acc = (outs[0] if ns == 1 else jnp.concatenate(outs, axis=0)) + p_ref[0:1, :]
    mu = jnp.mean(acc, axis=-1, keepdims=True)
    xc = acc - mu
    var = jnp.mean(xc * xc, axis=-1, keepdims=True)
    return _silu(xc * lax.rsqrt(var + 1e-5) * p_ref[1:2, :] + p_ref[2:3, :])


FFN_SPLITS = ((0, 1024), (1024, 2048), (2048, FFN_HIDDEN))


def _out_ffn_kernel(h_ref, ya_ref, yf_ref, cc_ref, p_ref, cbuf_ref, gw_ref, gb_ref, dw_ref, ccp_ref,
                    woa_ref, wob_ref, woc_ref, nf_ref, w1_ref, w3_ref, w2_ref, pew_ref, peg_ref, nfin_ref,
                    o_ref, nb_ref, yb_scr, ys_scr, xbuf, sh_scr, *, tm, ns, tps, final):
    nck = tm // S5_CHUNK
    gpl = LANES // S5_GROUP
    for s in range(S5_CHUNK):
        for g in range(S5_GROUPS):
            ys_scr[s, g // gpl, :, (g % gpl) * S5_GROUP:(g % gpl + 1) * S5_GROUP] = (
                yf_ref[0, g, :, s * S5_GROUP:(s + 1) * S5_GROUP])
        for half in range(S5_WIDTH // LANES):
            yb_scr[half, pl.ds(s, nck, stride=S5_CHUNK), :] = ys_scr[s, half]
    yb = jnp.concatenate([yb_scr[half] for half in range(S5_WIDTH // LANES)], axis=1)
    yb = yb * jax.nn.sigmoid(_dot(yb.astype(BF16), gw_ref[...]) + gb_ref[...])
    yc = _cc_module(cc_ref[...], cbuf_ref, dw_ref, ccp_ref, nb_ref, xbuf, sh_scr, ns=ns, rs=tm // ns,
                    first_tile=pl.program_id(0) % tps == 0)
    h = h_ref[...] + (_dot(ya_ref[...].astype(BF16), woa_ref[...])
                      + _dot(yb.astype(BF16), wob_ref[...])
                      + _dot(yc.astype(BF16), woc_ref[...]))
    ms = jnp.mean(h * h, axis=-1, keepdims=True)
    hf = ((h * lax.rsqrt(ms + 1e-6)) * nf_ref[...]).astype(BF16)
    for lo, hi in FFN_SPLITS:
        a = _dot(hf, w1_ref[:, lo:hi])
        b = _dot(hf, w3_ref[:, lo:hi])
        h = h + _dot((_silu(a) * b).astype(BF16), w2_ref[lo:hi, :])
    pe = _dot(p_ref[...].astype(BF16), pew_ref[...])
    h = h + pe * jax.nn.sigmoid(_dot(h.astype(BF16), peg_ref[...]))
    if final:
        ms = jnp.mean(h * h, axis=-1, keepdims=True)
        h = (h * lax.rsqrt(ms + 1e-6)) * nfin_ref[...]
    o_ref[...] = h


def _out_ffn(h, ya, yf, cc, p, cbuf, consts, tm, ns, tps, final):
    n = h.shape[0]
    row = lambda w: pl.BlockSpec((tm, w), lambda i: (i, 0))
    seq_blk = pl.BlockSpec((ns, CC_HIST, CC_WIDTH), lambda i: (i // tps, 0, 0))
    return pl.pallas_call(
        functools.partial(_out_ffn_kernel, tm=tm, ns=ns, tps=tps, final=final),
        grid=(n // tm,),
        in_specs=[row(D_MODEL), row(GDN_WIDTH), _s5_block_spec(tm, tps), row(2 * CC_WIDTH), row(PLE_DIM), seq_blk]
        + [_const_spec(a.shape) for a in consts],
        out_specs=[row(D_MODEL), seq_blk],
        out_shape=[jax.ShapeDtypeStruct((n, D_MODEL), F32), jax.ShapeDtypeStruct(cbuf.shape, F32)],
        scratch_shapes=[pltpu.VMEM((S5_WIDTH // LANES, tm, LANES), F32),
                        pltpu.VMEM((S5_CHUNK, S5_WIDTH // LANES, tm // S5_CHUNK, LANES), F32),
                        pltpu.VMEM((ns, tm // ns + CC_PAD, CC_WIDTH), F32),
                        pltpu.VMEM((7, tm // ns + CC_PAD - 8, CC_WIDTH), F32)],
        compiler_params=_cparams(("arbitrary",)),
        name="out_ffn",
    )(h, ya, yf, cc, p, cbuf, *consts)


def _tile_rows(n, want):
    t = min(n, want)
    while n % t:
        t //= 2
    return t


def _prep_layer(i, W):
    w_in = W['w_in'][i]
    L = {}
    L['g_mix'] = W['norm_mix'][i][None, :]
    L['wq'] = w_in[:, :OFF_Z].astype(BF16)
    L['wz'] = w_in[:, OFF_Z:OFF_BA].astype(BF16)
    L['wba'] = jnp.pad(w_in[:, OFF_BA:OFF_S5], ((0, 0), (0, LANES - 2 * GDN_HEADS))).astype(BF16)
    L['wu'] = w_in[:, OFF_S5:OFF_CC].astype(BF16)
    L['wc'] = w_in[:, OFF_CC:].astype(BF16)
    L['conv_w'] = W['gdn_conv_w'][i]
    padl = (GDN_HEADS, LANES - 2 * GDN_HEADS)
    L['ad'] = jnp.stack([jnp.pad(W['gdn_a_log'][i], padl), jnp.pad(W['gdn_dt_bias'][i], padl)])
    L['gdn_norm'] = W['gdn_norm'][i][None, :]
    lam = jnp.stack([W['s5_lam_re'][i], W['s5_lam_im'][i]])[:, :, None, :]
    bt =jnp.stack([jnp.swapaxes(W['s5_b_re'][i], 1, 2), jnp.swapaxes(W['s5_b_im'][i], 1, 2)])
    cm = jnp.stack([W['s5_c_re'][i], W['s5_c_im'][i]])
    ktoep, bend, cst, lamc = _s5_prep(lam, W['s5_log_dt'][i][:, None, None], bt, cm)
    w = S5_CHUNK * S5_GROUP
    L['ktoep'] = ktoep
    L['bend'] = bend.reshape(S5_GROUPS, w, 2 * S5_STATE)
    L['cst'] = cst.reshape(S5_GROUPS, w, 2 * S5_STATE)
    L['lamc'] = lamc.reshape(S5_GROUPS, 2 * S5_STATE)
    L['d_t'] = jnp.tile(W['s5_d'][i].reshape(S5_GROUPS, S5_GROUP), (1, S5_CHUNK))
    L['glu_w'] = W['s5_glu_w'][i].astype(BF16)
    L['glu_b'] = W['s5_glu_b'][i][None, :]
    L['dw_w'] = W['cc_dw_w'][i]
    L['cc_p'] = jnp.stack([W['cc_dw_b'][i], W['cc_ln_g'][i], W['cc_ln_b'][i]])
    wo = W['w_out'][i].astype(BF16)
    L['woa'], L['wob'], L['woc'] = wo[:GDN_WIDTH], wo[GDN_WIDTH:GDN_WIDTH + S5_WIDTH], wo[GDN_WIDTH + S5_WIDTH:]
    L['g_ffn'] = W['norm_ffn'][i][None, :]
    L['w1'] = W['ffn_w1'][i].astype(BF16)
    L['w3'] = W['ffn_w3'][i].astype(BF16)
    L['w2'] = W['ffn_w2'][i].astype(BF16)
    L['pew'] = W['pe_w'][i].astype(BF16)
    L['peg'] = W['pe_gate_w'][i].astype(BF16)
    L['g_fin'] = W['norm_final'][None, :]
    return L


def _layer(h, p, st_gdn, st_gconv, st_s5, st_conv, L, final):
    b, t, _ = h.shape
    n = b * t
    tm = _tile_rows(n, ROW_TILE)
    ns, tps = _seq_tiling(b, t, tm)
    qkv, z, ba, u_f, cc = _in_proj(h.reshape(n, D_MODEL), L['g_mix'], L['wq'], L['wz'], L['wba'], L['wu'], L['wc'],
                                   tm, tps)

    chunk = 64 if t % 64 == 0 else t
    tt = _tile_rows(t, 256)
    ya, s_gdn, b_gconv = _gdn(qkv.reshape(b, t, QKV_COLS), z.reshape(b, t, GDN_WIDTH), ba.reshape(b, t, LANES),
                              st_gconv, st_gdn, L['conv_w'], L['ad'], L['gdn_norm'], tt, chunk)

    h0 = jnp.concatenate([st_s5[..., 0], st_s5[..., 1]], axis=-1)
    h0 = jnp.swapaxes(h0.reshape(b // ns, ns, S5_GROUPS, 2 * S5_STATE), 1, 2)
    y_f, hn = _s5(u_f, h0, L['ktoep'], L['bend'], L['cst'], L['lamc'], L['d_t'], t // S5_CHUNK)
    hn = jnp.swapaxes(hn, 1, 2).reshape(b, S5_GROUPS, 2 * S5_STATE)
    s_s5 = jnp.stack([hn[..., :S5_STATE], hn[..., S5_STATE:]], axis=-1)

    consts = [L['glu_w'], L['glu_b'], L['dw_w'], L['cc_p'], L['woa'], L['wob'], L['woc'], L['g_ffn'],
              L['w1'], L['w3'], L['w2'], L['pew'], L['peg'], L['g_fin']]
    h_new, b_conv = _out_ffn(h.reshape(n, D_MODEL), ya.reshape(n, GDN_WIDTH), y_f, cc, p.reshape(n, PLE_DIM),
                             st_conv, consts, tm, ns, tps, final)
    return h_new.reshape(b, t, D_MODEL), s_gdn, b_gconv, s_s5, b_conv


def _trunk(x, p, st_gdn, st_gconv, st_s5, st_conv, layers):
    h = x
    outs = ([], [], [], [])
    depth = len(layers)
    for i, L in enumerate(layers):
        h, *st = _layer(h, p[i], st_gdn[i], st_gconv[i], st_s5[i], st_conv[i], L, i == depth - 1)
        for acc, s in zip(outs, st):
            acc.append(s)
    return (h,) + tuple(jnp.stack(o) for o in outs)


def kernel(x_prompt, x_sample, p_prompt, p_sample, state_gdn, state_gdn_conv, state_s5, state_conv, norm_mix, w_in, gdn_conv_w, gdn_a_log, gdn_dt_bias, gdn_norm, s5_lam_re, s5_lam_im, s5_log_dt, s5_b_re, s5_b_im, s5_c_re, s5_c_im, s5_d, s5_glu_w, s5_glu_b, cc_dw_w, cc_dw_b, cc_ln_g, cc_ln_b, w_out, norm_ffn, ffn_w1, ffn_w3, ffn_w2, pe_w, pe_gate_w, norm_final):
    W = dict(norm_mix=norm_mix, w_in=w_in, gdn_conv_w=gdn_conv_w, gdn_a_log=gdn_a_log, gdn_dt_bias=gdn_dt_bias,
             gdn_norm=gdn_norm, s5_lam_re=s5_lam_re, s5_lam_im=s5_lam_im, s5_log_dt=s5_log_dt, s5_b_re=s5_b_re,
             s5_b_im=s5_b_im, s5_c_re=s5_c_re, s5_c_im=s5_c_im, s5_d=s5_d, s5_glu_w=s5_glu_w, s5_glu_b=s5_glu_b,
             cc_dw_w=cc_dw_w, cc_dw_b=cc_dw_b, cc_ln_g=cc_ln_g, cc_ln_b=cc_ln_b, w_out=w_out, norm_ffn=norm_ffn,
             ffn_w1=ffn_w1, ffn_w3=ffn_w3, ffn_w2=ffn_w2, pe_w=pe_w, pe_gate_w=pe_gate_w, norm_final=norm_final)
    depth = w_in.shape[0]
    layers = [_prep_layer(i, W) for i in range(depth)]
    bp = x_prompt.shape[0]
    z_gdn = jnp.zeros((depth, bp, GDN_HEADS, GDN_HEAD_DIM, GDN_HEAD_DIM), F32)
    z_gconv = jnp.zeros((depth, bp, GDN_CONV - 1, QKV_COLS), F32)
    z_s5 = jnp.zeros((depth, bp, S5_GROUPS, S5_STATE, 2), F32)
    z_conv = jnp.zeros((depth, bp, CC_KERNEL - 1, CC_WIDTH), F32)
    y_p, gdn_p, gconv_p, s5_p, conv_p = _trunk(x_prompt, p_prompt, z_gdn, z_gconv, z_s5, z_conv, layers)
    y_s, gdn_s, gconv_s, s5_s, conv_s = _trunk(x_sample, p_sample, state_gdn, state_gdn_conv, state_s5,
                                               state_conv, layers)
    return (y_p, y_s, gdn_p, gconv_p, s5_p, conv_p, gdn_s, gconv_s, s5_s, conv_s)
```

```python
import functools
import math

import jax
import jax.numpy as jnp
from jax import lax
from jax.experimental import pallas as pl
from jax.experimental.pallas import tpu as pltpu

F32 = jnp.float32
BF16 = jnp.bfloat16

D_MODEL = 1024
GDN_HEADS = 4
GDN_HEAD_DIM = 128
GDN_WIDTH = GDN_HEADS * GDN_HEAD_DIM
GDN_CONV = 4
QKV_COLS = 3 * GDN_WIDTH
S5_WIDTH = 256
S5_GROUP = 16
S5_GROUPS = 16
S5_STATE = 64
S5_CHUNK = 16
S5_UNROLL = 4
CC_WIDTH = 256
CC_KERNEL = 31
FFN_HIDDEN = 2816
ROW_TILE = 512
PLE_DIM = 256
OFF_Z = QKV_COLS
OFF_BA = OFF_Z + GDN_WIDTH
OFF_S5 = OFF_BA + 2 * GDN_HEADS
OFF_CC = OFF_S5 + S5_WIDTH
IN_COLS = OFF_CC + 2 * CC_WIDTH
LANES = 128
VMEM_LIMIT = 56 * 1024 * 1024


def _cparams(sem):
    return pltpu.CompilerParams(dimension_semantics=sem, vmem_limit_bytes=VMEM_LIMIT)


def _const_spec(shape):
    nd = len(shape)
    return pl.BlockSpec(shape, lambda *_: (0,) * nd, pipeline_mode=pl.Buffered(1))


def _split2(x):
    hi = x.astype(BF16)
    lo = (x - hi.astype(F32)).astype(BF16)
    return hi, lo


def _split3(x):
    hi = x.astype(BF16)
    r = x - hi.astype(F32)
    mid = r.astype(BF16)
    lo = (r - mid.astype(F32)).astype(BF16)
    return hi, mid, lo


def _dot(a, b):
    return jnp.dot(a, b, preferred_element_type=F32)


def _dot_nt(a, b):
    return lax.dot_general(a, b, (((1,), (1,)), ((), ())), preferred_element_type=F32)


def _dot_tn(a, b):
    return lax.dot_general(a, b, (((0,), (0,)), ((), ())), preferred_element_type=F32)


def _dot_x3(a, b):
    ah, al = _split2(a)
    bh, bl = _split2(b)
    return _dot(ah, bh) + (_dot(ah, bl) + _dot(al, bh))


def _dot_exact_lhs(l_bf16, x):
    h, m, lo = _split3(x)
    return _dot(l_bf16, h) + (_dot(l_bf16, m) + _dot(l_bf16, lo))


def _silu(x):
    return x * jax.nn.sigmoid(x)


def _seq_tiling(b, t, tm):
    if t >= tm:
        assert t % tm == 0
        return 1, t // tm
    assert tm % t == 0 and (b * t) % tm == 0
    return tm // t, 1


def _s5_block_spec(tm, tps):
    return pl.BlockSpec((1, S5_GROUPS, tm // S5_CHUNK, S5_CHUNK * S5_GROUP), lambda i: (i // tps, 0, i % tps, 0))


def _inproj_kernel(h_ref, g_ref, cbuf_ref, wq_ref, wz_ref, wba_ref, wu_ref, wc_ref, dw_ref, ccp_ref,
                   qkv_ref, z_ref, ba_ref, uf_ref, yc_ref, nb_ref, u_scr, xbuf, sh_scr, *, tm, ns, tps):
    _cc_load_history(cbuf_ref, xbuf, ns=ns, rs=tm // ns, first_tile=pl.program_id(0) % tps == 0)
    x = h_ref[...]
    ms = jnp.mean(x * x, axis=-1, keepdims=True)
    hn = (x * lax.rsqrt(ms + 1e-6)) * g_ref[...]
    hb = hn.astype(BF16)
    u = _dot(hb, wu_ref[...])
    nck = tm // S5_CHUNK
    gpl = LANES // S5_GROUP
    for half in range(S5_WIDTH // LANES):
        u_scr[half] = u[:, half * LANES:(half + 1) * LANES]
    cc = _dot(hb, wc_ref[...])
    for s in range(S5_CHUNK):
        for half in range(S5_WIDTH // LANES):
            rows = u_scr[half, pl.ds(s, nck, stride=S5_CHUNK), :]
            for g in range(gpl):
                uf_ref[0, half * gpl + g, :, s * S5_GROUP:(s + 1) * S5_GROUP] = (
                    rows[:, g * S5_GROUP:(g + 1) * S5_GROUP])
    qkv_ref[...] = _dot(hb, wq_ref[...])
    z_ref[...] = _dot(hb, wz_ref[...])
    ba_ref[...] = _dot(hb, wba_ref[...])
    yc_ref[...] = _cc_module(cc, dw_ref, ccp_ref, nb_ref, xbuf, sh_scr, ns=ns, rs=tm // ns)


def _in_proj(h, g, cbuf, wq, wz, wba, wu, wc, dw_w, cc_p, tm, ns, tps):
    n = h.shape[0]
    row = lambda w: pl.BlockSpec((tm, w), lambda i: (i, 0))
    seq_blk = pl.BlockSpec((ns, CC_HIST, CC_WIDTH), lambda i: (i // tps, 0, 0))
    nblk = n // (tm * tps)
    rs = tm // ns
    consts = [wq, wz, wba, wu, wc, dw_w, cc_p]
    return pl.pallas_call(
        functools.partial(_inproj_kernel, tm=tm, ns=ns, tps=tps),
        grid=(n // tm,),
        in_specs=[row(D_MODEL), _const_spec((1, D_MODEL)), seq_blk] + [_const_spec(a.shape) for a in consts],
        out_specs=[row(QKV_COLS), row(GDN_WIDTH), row(LANES), _s5_block_spec(tm, tps), row(CC_WIDTH), seq_blk],
        out_shape=[jax.ShapeDtypeStruct((n, QKV_COLS), F32), jax.ShapeDtypeStruct((n, GDN_WIDTH), F32),
                   jax.ShapeDtypeStruct((n, LANES), F32),
                   jax.ShapeDtypeStruct((nblk, S5_GROUPS, tps * tm // S5_CHUNK, S5_CHUNK * S5_GROUP), F32),
                   jax.ShapeDtypeStruct((n, CC_WIDTH), F32), jax.ShapeDtypeStruct(cbuf.shape, F32)],
        scratch_shapes=[pltpu.VMEM((S5_WIDTH // LANES, tm, LANES), F32),
                        pltpu.VMEM((ns, rs + CC_PAD, CC_WIDTH), F32),
                        pltpu.VMEM((7, rs + CC_PAD - 8, CC_WIDTH), F32)],
        compiler_params=_cparams(("arbitrary",)),
        name="in_proj",
    )(h, g, cbuf, *consts)


def _gdn_kernel(qkv_ref, z_ref, ba_ref, buf_ref, s0_ref, cw_ref, ad_ref, nw_ref,
                ya_ref, sn_ref, nb_ref,
                xbuf, s_scr, q_scr, k_scr, v_scr, gb_scr, *, tt, chunk):
    ti = pl.program_id(1)
    nt = pl.num_programs(1)
    hd = GDN_HEAD_DIM
    nh = GDN_HEADS
    c = chunk
    pad = 8
    hist = GDN_CONV - 1

    @pl.when(ti == 0)
    def _():
        xbuf[pad - hist:pad, :] = buf_ref[0]
        s_scr[...] = s0_ref[0]

    @pl.when(ti > 0)
    def _():
        xbuf[pad - hist:pad, :] = xbuf[pad + tt - hist:pad + tt, :]

    xbuf[pad:pad + tt, :] = qkv_ref[0]
    nb_ref[0] = xbuf[pad + tt - hist:pad + tt, :]

    for cb in range(3 * nh):
        cols = slice(cb * hd, (cb + 1) * hd)
        acc = cw_ref[0:1, cols] * xbuf[pad - hist:pad - hist + tt, cols]
        for j in range(1, GDN_CONV):
            acc = acc + cw_ref[j:j + 1, cols] * xbuf[pad - hist + j:pad - hist + j + tt, cols]
        y = _silu(acc)
        h = cb % nh
        if cb < nh:
            q_scr[:, h * hd:(h + 1) * hd] = y * (lax.rsqrt(jnp.sum(y * y, axis=-1, keepdims=True) + 1e-6)
                                                 * (hd ** -0.5))
        elif cb < 2 * nh:
            k_scr[:, h * hd:(h + 1) * hd] = y * lax.rsqrt(jnp.sum(y * y, axis=-1, keepdims=True) + 1e-6)
        else:
            v_scr[:, h * hd:(h + 1) * hd] = y

    ba = ba_ref[0]
    beta = jax.nn.sigmoid(ba)
    sp_in = ba + ad_ref[1:2, :]
    softplus = jnp.maximum(sp_in, 0.0) + jnp.log(1.0 + jnp.exp(-jnp.abs(sp_in)))
    g = -jnp.exp(ad_ref[0:1, :]) * softplus
    gb_scr[0] = beta
    gb_scr[1] = g

    nck = tt // c
    t_r = lax.broadcasted_iota(jnp.int32, (tt, tt), 0)
    t_c = lax.broadcasted_iota(jnp.int32, (tt, tt), 1)
    tril_bd = ((t_r // c == t_c // c) & (t_r >= t_c)).astype(BF16)
    pi = lax.broadcasted_iota(jnp.int32, (tt, nh * c), 0) % c
    pc = lax.broadcasted_iota(jnp.int32, (tt, nh * c), 1)
    ps = pc % c
    p_head = pc // c
    ci_r = lax.broadcasted_iota(jnp.int32, (c, nh * c), 0)
    ci_s = lax.broadcasted_iota(jnp.int32, (c, nh * c), 1) % c
    strict4 = ci_r > ci_s
    causal4 = ci_r >= ci_s
    eye4 = (ci_r == ci_s).astype(F32)
    bd_r = lax.broadcasted_iota(jnp.int32, (nh * c, nh * c), 0) // c
    bd_c = lax.broadcasted_iota(jnp.int32, (nh * c, nh * c), 1) // c
    bd_mask = (bd_r == bd_c).astype(BF16)
    kbd_r = lax.broadcasted_iota(jnp.int32, (nh * c, GDN_WIDTH), 0) // c
    kbd_c = lax.broadcasted_iota(jnp.int32, (nh * c, GDN_WIDTH), 1) // hd
    kbd_mask = (kbd_r == kbd_c).astype(BF16)
    n_dbl = int(math.log2(c)) - 1

    def blockdiag(xb):
        return jnp.concatenate([xb] * nh, axis=0) * bd_mask

    def mm_packed(a4, b4):
        return _dot(a4.astype(BF16), blockdiag(b4.astype(BF16)))

    def pad_rows(x, h):
        z = jnp.zeros_like(x)
        return jnp.concatenate([x if j == h else z for j in range(nh)], axis=0)

    g_all = gb_scr[1]
    gc_all = _dot_exact_lhs(tril_bd, g_all)
    g4 = jnp.zeros((tt, nh * c), F32)
    for h in range(nh):
        g4 = jnp.where(p_head == h, jnp.broadcast_to(g_all[:, nh + h:nh + h + 1], (tt, nh * c)), g4)
    decay_all = jnp.exp(_dot_exact_lhs(tril_bd, jnp.where(pi > ps, g4, 0.0)))

    chunk_rows = [slice(ck * c, (ck + 1) * c) for ck in range(nck)]
    g_last_b = jnp.concatenate(
        [jnp.broadcast_to(gc_all[(ck + 1) * c - 1:(ck + 1) * c], (c, LANES)) for ck in range(nck)], axis=0)
    e_gc = jnp.exp(gc_all)
    e_rem = jnp.exp(g_last_b - gc_all)
    e_last = [jnp.exp(gc_all[(ck + 1) * c - 1:(ck + 1) * c]) for ck in range(nck)]
    beta_all = gb_scr[0]
    q_all = q_scr[...]
    k_all = k_scr[...]
    v_all = v_scr[...]
    kb_parts, wq_b, kd_b, rhs_b = [], [], [], []
    for h in range(nh):
        sl = slice(h * hd, (h + 1) * hd)
        b_h = jnp.broadcast_to(beta_all[:, h:h + 1], (tt, hd))
        eg_h = jnp.broadcast_to(e_gc[:, nh + h:nh + h + 1], (tt, hd))
        er_h = jnp.broadcast_to(e_rem[:, nh + h:nh + h + 1], (tt, hd))
        kb_h = k_all[:, sl] * b_h
        kb_parts.append(kb_h)
        wq_b.append((q_all[:, sl] * eg_h).astype(BF16))
        kd_b.append((k_all[:, sl] * er_h).astype(BF16))
        rhs_b.append(jnp.concatenate([v_all[:, sl] * b_h, kb_h * eg_h], axis=1).astype(BF16))
    kb_b = jnp.concatenate(kb_parts, axis=1).astype(BF16)
    q_b = q_all.astype(BF16)
    k_b = k_all.astype(BF16)

    kq = []
    for rows in chunk_rows:
        k_bd = jnp.concatenate([k_b[rows]] * nh, axis=0) * kbd_mask
        kq.append(_dot_nt(jnp.concatenate([kb_b[rows], q_b[rows]], axis=0), k_bd))
    x4, t4, aqk4 = [], [], []
    for ck, rows in enumerate(chunk_rows):
        n4 = jnp.where(strict4, kq[ck][0:c] * decay_all[rows], 0.0)
        aqk4.append(jnp.where(causal4, kq[ck][c:2 * c] * decay_all[rows], 0.0).astype(BF16))
        x4.append(n4)
        t4.append(eye4 - n4)
    for _ in range(n_dbl):
        x4 = [mm_packed(x, x) for x in x4]
        t4 = [t + mm_packed(t, x) for t, x in zip(t4, x4)]
    uw = []
    for ck, rows in enumerate(chunk_rows):
        t4b = t4[ck].astype(BF16)
        uw.append([_dot(t4b, pad_rows(rhs_b[h][rows], h)) for h in range(nh)])

    s_cur = [s_scr[h] for h in range(nh)]
    for ck, rows in enumerate(chunk_rows):
        ws = [_dot(jnp.concatenate([uw[ck][h][:, hd:2 * hd].astype(BF16), wq_b[h][rows]], axis=0),
                   s_cur[h].astype(BF16)) for h in range(nh)]
        v_new = [(uw[ck][h][:, 0:hd] - ws[h][0:c]).astype(BF16) for h in range(nh)]
        s_add = [_dot_tn(kd_b[h][rows], v_new[h]) for h in range(nh)]
        o_att = [_dot(aqk4[ck], pad_rows(v_new[h], h)) for h in range(nh)]
        for h in range(nh):
            sl = slice(h * hd, (h + 1) * hd)
            dl = jnp.broadcast_to(e_last[ck][:, nh + h:nh + h + 1], (hd, hd))
            s_cur[h] = s_cur[h] * dl + s_add[h]
            o_h = ws[h][c:2 * c] + o_att[h]
            o_n = o_h * lax.rsqrt(jnp.mean(o_h * o_h, axis=-1, keepdims=True) + 1e-6) * nw_ref[...]
            ya_ref[0, rows, sl] = o_n * _silu(z_ref[0, rows, sl])
    for h in range(nh):
        s_scr[h] = s_cur[h]

    @pl.when(ti == nt - 1)
    def _():
        sn_ref[0] = s_scr[...]


def _gdn(qkv, z, ba, buf, s0, conv_w, ad, norm_w, tt, chunk):
    b, t, _ = qkv.shape
    kern = functools.partial(_gdn_kernel, tt=tt, chunk=chunk)
    tile = lambda w: pl.BlockSpec((1, tt, w), lambda i, j: (i, j, 0))
    per_b3 = lambda s: pl.BlockSpec((1,) + s, lambda i, j: (i,) + (0,) * len(s))
    return pl.pallas_call(
        kern,
        grid=(b, t // tt),
        in_specs=[tile(QKV_COLS), tile(GDN_WIDTH), tile(LANES), per_b3((GDN_CONV - 1, QKV_COLS)),
                  per_b3((GDN_HEADS, GDN_HEAD_DIM, GDN_HEAD_DIM)), _const_spec(conv_w.shape),
                  _const_spec(ad.shape), _const_spec(norm_w.shape)],
        out_specs=[tile(GDN_WIDTH), per_b3((GDN_HEADS, GDN_HEAD_DIM, GDN_HEAD_DIM)),
                   per_b3((GDN_CONV - 1, QKV_COLS))],
        out_shape=[jax.ShapeDtypeStruct((b, t, GDN_WIDTH), F32),
                   jax.ShapeDtypeStruct((b, GDN_HEADS, GDN_HEAD_DIM, GDN_HEAD_DIM), F32),
                   jax.ShapeDtypeStruct((b, GDN_CONV - 1, QKV_COLS), F32)],
        scratch_shapes=[pltpu.VMEM((tt + 8, QKV_COLS), F32),
                        pltpu.VMEM((GDN_HEADS, GDN_HEAD_DIM, GDN_HEAD_DIM), F32),
                        pltpu.VMEM((tt, GDN_WIDTH), F32), pltpu.VMEM((tt, GDN_WIDTH), F32),
                        pltpu.VMEM((tt, GDN_WIDTH), F32), pltpu.VMEM((2, tt, LANES), F32)],
        compiler_params=_cparams(("parallel", "arbitrary")),
        name="gdn",
    )(qkv, z, ba, buf, s0, conv_w, ad, norm_w)


def _s5_prep_kernel(lam_ref, ldt_ref, bt_ref, cm_ref, ktoep_ref, bend_ref, cst_ref, lamc_ref):
    g_n, p_n, lc = S5_GROUPS, S5_STATE, S5_CHUNK
    lr, li = lam_ref[0], lam_ref[1]
    dt = jnp.exp(ldt_ref[...])
    mag = jnp.exp(lr * dt)
    ang = li * dt
    br, bi = mag * jnp.cos(ang), mag * jnp.sin(ang)
    den = lr * lr + li * li
    qr = ((br - 1.0) * lr + bi * li) / den
    qi = (bi * lr - (br - 1.0) * li) / den
    bbr = qr * bt_ref[0] - qi * bt_ref[1]
    bbi = qr * bt_ref[1] + qi * bt_ref[0]
    cr, cim = cm_ref[0], cm_ref[1]

    pw = [(jnp.ones_like(br), jnp.zeros_like(br))]
    for _ in range(lc):
        ar, ai = pw[-1]
        pw.append((ar * br - ai * bi, ar * bi + ai * br))
    lamc_ref[:, :, 0:p_n] = pw[lc][0]
    lamc_ref[:, :, p_n:2 * p_n] = pw[lc][1]

    for s in range(lc):
        ar, ai = pw[lc - 1 - s]
        bend_ref[:, s, :, 0:p_n] = ar * bbr - ai * bbi
        bend_ref[:, s, :, p_n:2 * p_n] = ar * bbi + ai * bbr
        ar, ai = pw[s + 1]
        cst_ref[:, s, :, 0:p_n] = ar * cr - ai * cim
        cst_ref[:, s, :, p_n:2 * p_n] = -(ar * cim + ai * cr)

    lane = lax.broadcasted_iota(jnp.int32, (S5_GROUP, lc * S5_GROUP), 1)
    for g in range(g_n):
        cpr = jnp.concatenate([pw[j][0][g] * cr[g] - pw[j][1][g] * cim[g] for j in range(lc)], axis=0)
        cpi = jnp.concatenate([pw[j][0][g] * cim[g] + pw[j][1][g] * cr[g] for j in range(lc)], axis=0)
        bh_r, bl_r = _split2(bbr[g])
        bh_i, bl_i = _split2(bbi[g])
        ch_r, cl_r = _split2(cpr)
        ch_i, cl_i = _split2(cpi)
        krow = (_dot_nt(bh_r, ch_r) + (_dot_nt(bh_r, cl_r) + _dot_nt(bl_r, ch_r))
                - (_dot_nt(bh_i, ch_i) + (_dot_nt(bh_i, cl_i) + _dot_nt(bl_i, ch_i))))
        for s in range(lc):
            sh = s * S5_GROUP
            blk = krow if s == 0 else jnp.where(lane >= sh, pltpu.roll(krow, sh, 1), 0.0)
            ktoep_ref[g, s * S5_GROUP:(s + 1) * S5_GROUP, :] = blk.astype(BF16)


def _s5_prep(lam, ldt, bt, cm):
    g_n, p_n, lc = S5_GROUPS, S5_STATE, S5_CHUNK
    w = lc * S5_GROUP
    return pl.pallas_call(
        _s5_prep_kernel,
        out_shape=[jax.ShapeDtypeStruct((g_n, w, w), BF16),
                   jax.ShapeDtypeStruct((g_n, lc, S5_GROUP, 2 * p_n), F32),
                   jax.ShapeDtypeStruct((g_n, lc, S5_GROUP, 2 * p_n), F32),
                   jax.ShapeDtypeStruct((g_n, 1, 2 * p_n), F32)],
        compiler_params=pltpu.CompilerParams(vmem_limit_bytes=VMEM_LIMIT),
        name="s5_prep",
    )(lam, ldt, bt, cm)


def _s5_kernel(u_ref, h0_ref, ktoep_ref, bend_ref, cst_ref, lamc_ref, d_ref, y_ref, hn_ref, *, n, nb):
    p_n = S5_STATE
    r_n = nb * n
    lane = lax.broadcasted_iota(jnp.int32, (1, 2 * p_n), 1)
    sign = jnp.where(lane < p_n, -1.0, 1.0)
    rowm = lax.broadcasted_iota(jnp.int32, (r_n, 2 * p_n), 0) % n
    if nb > 1:
        e_r = lax.broadcasted_iota(jnp.int32, (r_n, nb), 0)
        e_b = lax.broadcasted_iota(jnp.int32, (r_n, nb), 1)
        expand = (e_r // n == e_b).astype(BF16)
        s_b = lax.broadcasted_iota(jnp.int32, (nb, r_n), 0)
        s_r = lax.broadcasted_iota(jnp.int32, (nb, r_n), 1)
        pick_last = (s_r == s_b * n + (n - 1)).astype(BF16)

    def lam_tiles(g):
        lam = lamc_ref[pl.ds(g, 1), :]
        lam_sw = pltpu.roll(lam, p_n, 1)
        return (jnp.where(lane < p_n, lam, lam_sw),
                jnp.where(lane < p_n, lam_sw, lam) * sign)

    def cmul(x, a_rr, a_is):
        return x * a_rr + pltpu.roll(x, p_n, 1) * a_is

    def groups_body(gi, carry):
        gs = [gi * S5_UNROLL + j for j in range(S5_UNROLL)]
        us = [u_ref[0, g] for g in gs]
        lams = [lam_tiles(g) for g in gs]
        x_loc = [_dot_x3(u, bend_ref[g]) for u, g in zip(us, gs)]
        xs = []
        for g, xl in zip(gs, x_loc):
            h0 = h0_ref[0, g]
            h0_rows = _dot_exact_lhs(expand, h0) if nb > 1 else h0
            xs.append(jnp.where(rowm == 0, h0_rows, pltpu.roll(xl, 1, 0)))
        pw = list(lams)
        step = 1
        while step < n:
            xs = [x + cmul(jnp.where(rowm >= step, pltpu.roll(x, step, 0), 0.0), a_rr, a_is)
                  for x, (a_rr, a_is) in zip(xs, pw)]
            pw = [(a_rr * a_rr - a_is * a_is, 2.0 * a_rr * a_is) for a_rr, a_is in pw]
            step *= 2
        for g, u, x, xl, (b_rr, b_is) in zip(gs, us, xs, x_loc, lams):
            y = _dot(u.astype(BF16), ktoep_ref[g]) + _dot_nt(x.astype(BF16), cst_ref[g].astype(BF16))
            y = y + d_ref[pl.ds(g, 1), :] * u
            y_ref[0, g] = jax.nn.gelu(y)
            x_end = cmul(x, b_rr, b_is) + xl
            hn_ref[0, g] = _dot_exact_lhs(pick_last, x_end) if nb > 1 else x_end[n - 1:n, :]
        return carry

    lax.fori_loop(0, S5_GROUPS // S5_UNROLL, groups_body, 0)


def _s5(u, h0, ktoep, bend, cst, lamc, d_t, n):
    nblk, g_n, r_n, w = u.shape
    nb = r_n // n
    p2 = 2 * S5_STATE
    kern = functools.partial(_s5_kernel, n=n, nb=nb)
    return pl.pallas_call(
        kern,
        grid=(nblk,),
        in_specs=[pl.BlockSpec((1, g_n, r_n, w), lambda i: (i, 0, 0, 0)),
                  pl.BlockSpec((1, g_n, nb, p2), lambda i: (i, 0, 0, 0)),
                  _const_spec(ktoep.shape), _const_spec(bend.shape), _const_spec(cst.shape),
                  _const_spec(lamc.shape), _const_spec(d_t.shape)],
        out_specs=[pl.BlockSpec((1, g_n, r_n, w), lambda i: (i, 0, 0, 0)),
                   pl.BlockSpec((1, g_n, nb, p2), lambda i: (i, 0, 0, 0))],
        out_shape=[jax.ShapeDtypeStruct((nblk, g_n, r_n, w), F32),
                   jax.ShapeDtypeStruct((nblk, g_n, nb, p2), F32)],
        compiler_params=_cparams(("parallel",)),
        name="s5",
    )(u, h0, ktoep, bend, cst, lamc, d_t)


CC_HIST = CC_KERNEL - 1
CC_PAD = 32


def _cc_load_history(buf_ref, xbuf, *, ns, rs, first_tile):
    lo = CC_PAD - CC_HIST
    if ns == 1:
        @pl.when(first_tile)
        def _():
            xbuf[0, lo:CC_PAD, :] = buf_ref[0]

        @pl.when(jnp.logical_not(first_tile))
        def _():
            xbuf[0, lo:CC_PAD, :] = xbuf[0, rs + lo:rs + CC_PAD, :]
    else:
        for q in range(ns):
            xbuf[q, lo:CC_PAD, :] = buf_ref[q]


def _cc_module(cc, w_ref, p_ref, nb_ref, xbuf, sh_scr, *, ns, rs):
    x = cc[:, 0:CC_WIDTH] * jax.nn.sigmoid(cc[:, CC_WIDTH:])
    lo = CC_PAD - CC_HIST
    outs = []
    for q in range(ns):
        xbuf[q, CC_PAD:CC_PAD + rs, :] = x[q * rs:(q + 1) * rs]
        nb_ref[q] = xbuf[q, rs + lo:rs + CC_PAD, :]
        for a in range(1, 8):
            sh_scr[a - 1] = xbuf[q, a:a + rs + CC_PAD - 8, :]
        acc = None
        for j in range(CC_KERNEL):
            a, m = (lo + j) % 8, (lo + j) // 8
            win = sh_scr[a - 1, 8 * m:8 * m + rs, :] if a else xbuf[q, 8 * m:8 * m + rs, :]
            term = w_ref[j:j + 1, :] * win
            acc = term if acc is None else acc + term
        outs.append(acc)
    acc = (outs[0] if ns == 1 else jnp.concatenate(outs, axis=0)) + p_ref[0:1, :]
    mu = jnp.mean(acc, axis=-1, keepdims=True)
    xc = acc - mu
    var = jnp.mean(xc * xc, axis=-1, keepdims=True)
    return _silu(xc * lax.rsqrt(var + 1e-5) * p_ref[1:2, :] + p_ref[2:3, :])


FFN_SPLITS = ((0, 1024), (1024, 2048), (2048, FFN_HIDDEN))


def _out_ffn_kernel(h_ref, ya_ref, yf_ref, yc_ref, p_ref, gw_ref, gb_ref,
                    woa_ref, wob_ref, woc_ref, nf_ref, w1_ref, w3_ref, w2_ref, pew_ref, peg_ref, nfin_ref,
                    o_ref, yb_scr, ys_scr, *, tm, final):
    nck = tm // S5_CHUNK
    gpl = LANES // S5_GROUP
    for s in range(S5_CHUNK):
        for g in range(S5_GROUPS):
            ys_scr[s, g // gpl, :, (g % gpl) * S5_GROUP:(g % gpl + 1) * S5_GROUP] = (
                yf_ref[0, g, :, s * S5_GROUP:(s + 1) * S5_GROUP])
        for half in range(S5_WIDTH // LANES):
            yb_scr[half, pl.ds(s, nck, stride=S5_CHUNK), :] = ys_scr[s, half]
    yb = jnp.concatenate([yb_scr[half] for half in range(S5_WIDTH // LANES)], axis=1)
    yb = yb * jax.nn.sigmoid(_dot(yb.astype(BF16), gw_ref[...]) + gb_ref[...])
    h = h_ref[...] + (_dot(ya_ref[...].astype(BF16), woa_ref[...])
                      + _dot(yb.astype(BF16), wob_ref[...])
                      + _dot(yc_ref[...].astype(BF16), woc_ref[...]))
    ms = jnp.mean(h * h, axis=-1, keepdims=True)
    hf = ((h * lax.rsqrt(ms + 1e-6)) * nf_ref[...]).astype(BF16)
    for lo, hi in FFN_SPLITS:
        a = _dot(hf, w1_ref[:, lo:hi])
        b = _dot(hf, w3_ref[:, lo:hi])
        h = h + _dot((_silu(a) * b).astype(BF16), w2_ref[lo:hi, :])
    pe = _dot(p_ref[...].astype(BF16), pew_ref[...])
    h = h + pe * jax.nn.sigmoid(_dot(h.astype(BF16), peg_ref[...]))
    if final:
        ms = jnp.mean(h * h, axis=-1, keepdims=True)
        h = (h * lax.rsqrt(ms + 1e-6)) * nfin_ref[...]
    o_ref[...] = h


def _out_ffn(h, ya, yf, yc, p, consts, tm, tps, final):
    n = h.shape[0]
    row = lambda w: pl.BlockSpec((tm, w), lambda i: (i, 0))
    return pl.pallas_call(
        functools.partial(_out_ffn_kernel, tm=tm, final=final),
        grid=(n // tm,),
        in_specs=[row(D_MODEL), row(GDN_WIDTH), _s5_block_spec(tm, tps), row(CC_WIDTH), row(PLE_DIM)]
        + [_const_spec(a.shape) for a in consts],
        out_specs=row(D_MODEL),
        out_shape=jax.ShapeDtypeStruct((n, D_MODEL), F32),
        scratch_shapes=[pltpu.VMEM((S5_WIDTH // LANES, tm, LANES), F32),
                        pltpu.VMEM((S5_CHUNK, S5_WIDTH // LANES, tm // S5_CHUNK, LANES), F32)],
        compiler_params=_cparams(("parallel",)),
        name="out_ffn",
    )(h, ya, yf, yc, p, *consts)


def _tile_rows(n, want):
    t = min(n, want)
    while n % t:
        t //= 2
    return t


def _prep_layer(i, W):
    w_in = W['w_in'][i]
    L = {}
    L['g_mix'] = W['norm_mix'][i][None, :]
    L['wq'] = w_in[:, :OFF_Z].astype(BF16)
    L['wz'] = w_in[:, OFF_Z:OFF_BA].astype(BF16)
    L['wba'] = jnp.pad(w_in[:, OFF_BA:OFF_S5], ((0, 0), (0, LANES - 2 * GDN_HEADS))).astype(BF16)
    L['wu'] = w_in[:, OFF_S5:OFF_CC].astype(BF16)
    L['wc'] = w_in[:, OFF_CC:].astype(BF16)
    L['conv_w'] = W['gdn_conv_w'][i]
    padl = (GDN_HEADS, LANES - 2 * GDN_HEADS)
    L['ad'] = jnp.stack([jnp.pad(W['gdn_a_log'][i], padl), jnp.pad(W['gdn_dt_bias'][i], padl)])
    L['gdn_norm'] = W['gdn_norm'][i][None, :]
    lam = jnp.stack([W['s5_lam_re'][i], W['s5_lam_im'][i]])[:, :, None, :]
    bt =jnp.stack([jnp.swapaxes(W['s5_b_re'][i], 1, 2), jnp.swapaxes(W['s5_b_im'][i], 1, 2)])
    cm = jnp.stack([W['s5_c_re'][i], W['s5_c_im'][i]])
    ktoep, bend, cst, lamc = _s5_prep(lam, W['s5_log_dt'][i][:, None, None], bt, cm)
    w = S5_CHUNK * S5_GROUP
    L['ktoep'] = ktoep
    L['bend'] = bend.reshape(S5_GROUPS, w, 2 * S5_STATE)
    L['cst'] = cst.reshape(S5_GROUPS, w, 2 * S5_STATE)
    L['lamc'] = lamc.reshape(S5_GROUPS, 2 * S5_STATE)
    L['d_t'] = jnp.tile(W['s5_d'][i].reshape(S5_GROUPS, S5_GROUP), (1, S5_CHUNK))
    L['glu_w'] = W['s5_glu_w'][i].astype(BF16)
    L['glu_b'] = W['s5_glu_b'][i][None, :]
    L['dw_w'] = W['cc_dw_w'][i]
    L['cc_p'] = jnp.stack([W['cc_dw_b'][i], W['cc_ln_g'][i], W['cc_ln_b'][i]])
    wo = W['w_out'][i].astype(BF16)
    L['woa'], L['wob'], L['woc'] = wo[:GDN_WIDTH], wo[GDN_WIDTH:GDN_WIDTH + S5_WIDTH], wo[GDN_WIDTH + S5_WIDTH:]
    L['g_ffn'] = W['norm_ffn'][i][None, :]
    L['w1'] = W['ffn_w1'][i].astype(BF16)
    L['w3'] = W['ffn_w3'][i].astype(BF16)
    L['w2'] = W['ffn_w2'][i].astype(BF16)
    L['pew'] = W['pe_w'][i].astype(BF16)
    L['peg'] = W['pe_gate_w'][i].astype(BF16)
    L['g_fin'] = W['norm_final'][None, :]
    return L


def _layer(h, p, st_gdn, st_gconv, st_s5, st_conv, L, final):
    b, t, _ = h.shape
    n = b * t
    tm = _tile_rows(n, ROW_TILE)
    ns, tps = _seq_tiling(b, t, tm)
    qkv, z, ba, u_f, yc, b_conv = _in_proj(h.reshape(n, D_MODEL), L['g_mix'], st_conv, L['wq'], L['wz'], L['wba'],
                                           L['wu'], L['wc'], L['dw_w'], L['cc_p'], tm, ns, tps)

    chunk = 64 if t % 64 == 0 else t
    tt = _tile_rows(t, 256)
    ya, s_gdn, b_gconv = _gdn(qkv.reshape(b, t, QKV_COLS), z.reshape(b, t, GDN_WIDTH), ba.reshape(b, t, LANES),
                              st_gconv, st_gdn, L['conv_w'], L['ad'], L['gdn_norm'], tt, chunk)

    h0 = jnp.concatenate([st_s5[..., 0], st_s5[..., 1]], axis=-1)
    h0 = jnp.swapaxes(h0.reshape(b // ns, ns, S5_GROUPS, 2 * S5_STATE), 1, 2)
    y_f, hn = _s5(u_f, h0, L['ktoep'], L['bend'], L['cst'], L['lamc'], L['d_t'], t // S5_CHUNK)
    hn = jnp.swapaxes(hn, 1, 2).reshape(b, S5_GROUPS, 2 * S5_STATE)
    s_s5 = jnp.stack([hn[..., :S5_STATE], hn[..., S5_STATE:]], axis=-1)

    consts = [L['glu_w'], L['glu_b'], L['woa'], L['wob'], L['woc'], L['g_ffn'],
              L['w1'], L['w3'], L['w2'], L['pew'], L['peg'], L['g_fin']]
    h_new = _out_ffn(h.reshape(n, D_MODEL), ya.reshape(n, GDN_WIDTH), y_f, yc, p.reshape(n, PLE_DIM),
                     consts, tm, tps, final)
    return h_new.reshape(b, t, D_MODEL), s_gdn, b_gconv, s_s5, b_conv


def _trunk(x, p, st_gdn, st_gconv, st_s5, st_conv, layers):
    h = x
    outs = ([], [], [], [])
    depth = len(layers)
    for i, L in enumerate(layers):
        h, *st = _layer(h, p[i], st_gdn[i], st_gconv[i], st_s5[i], st_conv[i], L, i == depth - 1)
        for acc, s in zip(outs, st):
            acc.append(s)
    return (h,) + tuple(jnp.stack(o) for o in outs)


def kernel(x_prompt, x_sample, p_prompt, p_sample, state_gdn, state_gdn_conv, state_s5, state_conv, norm_mix, w_in, gdn_conv_w, gdn_a_log, gdn_dt_bias, gdn_norm, s5_lam_re, s5_lam_im, s5_log_dt, s5_b_re, s5_b_im, s5_c_re, s5_c_im, s5_d, s5_glu_w, s5_glu_b, cc_dw_w, cc_dw_b, cc_ln_g, cc_ln_b, w_out, norm_ffn, ffn_w1, ffn_w3, ffn_w2, pe_w, pe_gate_w, norm_final):
    W = dict(norm_mix=norm_mix, w_in=w_in, gdn_conv_w=gdn_conv_w, gdn_a_log=gdn_a_log, gdn_dt_bias=gdn_dt_bias,
             gdn_norm=gdn_norm, s5_lam_re=s5_lam_re, s5_lam_im=s5_lam_im, s5_log_dt=s5_log_dt, s5_b_re=s5_b_re,
             s5_b_im=s5_b_im, s5_c_re=s5_c_re, s5_c_im=s5_c_im, s5_d=s5_d, s5_glu_w=s5_glu_w, s5_glu_b=s5_glu_b,
             cc_dw_w=cc_dw_w, cc_dw_b=cc_dw_b, cc_ln_g=cc_ln_g, cc_ln_b=cc_ln_b, w_out=w_out, norm_ffn=norm_ffn,
             ffn_w1=ffn_w1, ffn_w3=ffn_w3, ffn_w2=ffn_w2, pe_w=pe_w, pe_gate_w=pe_gate_w, norm_final=norm_final)
    depth = w_in.shape[0]
    layers = [_prep_layer(i, W) for i in range(depth)]
    bp = x_prompt.shape[0]
    z_gdn = jnp.zeros((depth, bp, GDN_HEADS, GDN_HEAD_DIM, GDN_HEAD_DIM), F32)
    z_gconv = jnp.zeros((depth, bp, GDN_CONV - 1, QKV_COLS), F32)
    z_s5 = jnp.zeros((depth, bp, S5_GROUPS, S5_STATE, 2), F32)
    z_conv = jnp.zeros((depth, bp, CC_KERNEL - 1, CC_WIDTH), F32)
    y_p, gdn_p, gconv_p, s5_p, conv_p = _trunk(x_prompt, p_prompt, z_gdn, z_gconv, z_s5, z_conv, layers)
    y_s, gdn_s, gconv_s, s5_s, conv_s = _trunk(x_sample, p_sample, state_gdn, state_gdn_conv, state_s5,
                                               state_conv, layers)
    return (y_p, y_s, gdn_p, gconv_p, s5_p, conv_p, gdn_s, gconv_s, s5_s, conv_s)
```

```python
import functools
import math

import jax
import jax.numpy as jnp
from jax import lax
from jax.experimental import pallas as pl
from jax.experimental.pallas import tpu as pltpu

F32 = jnp.float32
BF16 = jnp.bfloat16

D_MODEL = 1024
GDN_HEADS = 4
GDN_HEAD_DIM = 128
GDN_WIDTH = GDN_HEADS * GDN_HEAD_DIM
GDN_CONV = 4
QKV_COLS = 3 * GDN_WIDTH
S5_WIDTH = 256
S5_GROUP = 16
S5_GROUPS = 16
S5_STATE = 64
S5_CHUNK = 16
S5_UNROLL = 4
CC_WIDTH = 256
CC_KERNEL = 31
FFN_HIDDEN = 2816
ROW_TILE = 512
PLE_DIM = 256
OFF_Z = QKV_COLS
OFF_BA = OFF_Z + GDN_WIDTH
OFF_S5 = OFF_BA + 2 * GDN_HEADS
OFF_CC = OFF_S5 + S5_WIDTH
IN_COLS = OFF_CC + 2 * CC_WIDTH
LANES = 128
VMEM_LIMIT = 56 * 1024 * 1024


def _cparams(sem):
    return pltpu.CompilerParams(dimension_semantics=sem, vmem_limit_bytes=VMEM_LIMIT)


def _const_spec(shape):
    nd = len(shape)
    return pl.BlockSpec(shape, lambda *_: (0,) * nd, pipeline_mode=pl.Buffered(1))


def _split2(x):
    hi = x.astype(BF16)
    lo = (x - hi.astype(F32)).astype(BF16)
    return hi, lo


def _split3(x):
    hi = x.astype(BF16)
    r = x - hi.astype(F32)
    mid = r.astype(BF16)
    lo = (r - mid.astype(F32)).astype(BF16)
    return hi, mid, lo


def _dot(a, b):
    return jnp.dot(a, b, preferred_element_type=F32)


def _dot_nt(a, b):
    return lax.dot_general(a, b, (((1,), (1,)), ((), ())), preferred_element_type=F32)


def _dot_tn(a, b):
    return lax.dot_general(a, b, (((0,), (0,)), ((), ())), preferred_element_type=F32)


def _dot_x3(a, b):
    ah, al = _split2(a)
    bh, bl = _split2(b)
    return _dot(ah, bh) + (_dot(ah, bl) + _dot(al, bh))


def _dot_exact_lhs(l_bf16, x):
    h, m, lo = _split3(x)
    return _dot(l_bf16, h) + (_dot(l_bf16, m) + _dot(l_bf16, lo))


def _silu(x):
    return x * jax.nn.sigmoid(x)


def _seq_tiling(b, t, tm):
    if t >= tm:
        assert t % tm == 0
        return 1, t // tm
    assert tm % t == 0 and (b * t) % tm == 0
    return tm // t, 1


def _s5_block_spec(tm, tps):
    return pl.BlockSpec((1, S5_GROUPS, tm // S5_CHUNK, S5_CHUNK * S5_GROUP), lambda i: (i // tps, 0, i % tps, 0))


GDN_PAD = 8
GDN_HIST = GDN_CONV - 1


def _gdn_load_history(buf_ref, gx, *, ns, rs, first_tile):
    lo = GDN_PAD - GDN_HIST
    if ns == 1:
        @pl.when(first_tile)
        def _():
            gx[0, lo:GDN_PAD, :] = buf_ref[0]

        @pl.when(jnp.logical_not(first_tile))
        def _():
            gx[0, lo:GDN_PAD, :] = gx[0, rs + lo:rs + GDN_PAD, :]
    else:
        for q in range(ns):
            gx[q, lo:GDN_PAD, :] = buf_ref[q]


def _gdn_frontend(part, pre, cw_ref, out_ref, nb_ref, gx, *, ns, rs):
    hd, nh = GDN_HEAD_DIM, GDN_HEADS
    lo = GDN_PAD - GDN_HIST
    base = part * GDN_WIDTH
    for q in range(ns):
        rows = slice(q * rs, (q + 1) * rs)
        gx[q, GDN_PAD:GDN_PAD + rs, base:base + GDN_WIDTH] = pre[rows]
        nb_ref[q, :, base:base + GDN_WIDTH] = gx[q, rs + lo:rs + GDN_PAD, base:base + GDN_WIDTH]
        for h in range(nh):
            cols = slice(base + h * hd, base + (h + 1) * hd)
            acc = cw_ref[0:1, cols] * gx[q, lo:lo + rs, cols]
            for j in range(1, GDN_CONV):
                acc = acc + cw_ref[j:j + 1, cols] * gx[q, lo + j:lo + j + rs, cols]
            y = _silu(acc)
            if part == 0:
                y = y * (lax.rsqrt(jnp.sum(y * y, axis=-1, keepdims=True) + 1e-6) * (hd ** -0.5))
            elif part == 1:
                y = y * lax.rsqrt(jnp.sum(y * y, axis=-1, keepdims=True) + 1e-6)
            out_ref[rows, h * hd:(h + 1) * hd] = y


def _inproj_kernel(h_ref, g_ref, cbuf_ref, gbuf_ref, wq_ref, wz_ref, wba_ref, wu_ref, wc_ref, dw_ref, ccp_ref, cw_ref,
                   q_ref, k_ref, v_ref, z_ref, ba_ref, uf_ref, yc_ref, nb_ref, ngb_ref,
                   u_scr, xbuf, sh_scr, gx, *, tm, ns, tps):
    first_tile = pl.program_id(0) % tps == 0
    _cc_load_history(cbuf_ref, xbuf, ns=ns, rs=tm // ns, first_tile=first_tile)
    _gdn_load_history(gbuf_ref, gx, ns=ns, rs=tm // ns, first_tile=first_tile)
    x = h_ref[...]
    ms = jnp.mean(x * x, axis=-1, keepdims=True)
    hn = (x * lax.rsqrt(ms + 1e-6)) * g_ref[...]
    hb = hn.astype(BF16)
    u = _dot(hb, wu_ref[...])
    nck = tm // S5_CHUNK
    gpl = LANES // S5_GROUP
    for half in range(S5_WIDTH // LANES):
        u_scr[half] = u[:, half * LANES:(half + 1) * LANES]
    cc = _dot(hb, wc_ref[...])
    for s in range(S5_CHUNK):
        for half in range(S5_WIDTH // LANES):
            rows = u_scr[half, pl.ds(s, nck, stride=S5_CHUNK), :]
            for g in range(gpl):
                uf_ref[0, half * gpl + g, :, s * S5_GROUP:(s + 1) * S5_GROUP] = (
                    rows[:, g * S5_GROUP:(g + 1) * S5_GROUP])
    q_pre = _dot(hb, wq_ref[:, 0:GDN_WIDTH])
    yc_ref[...] = _cc_module(cc, dw_ref, ccp_ref, nb_ref, xbuf, sh_scr, ns=ns, rs=tm // ns)
    k_pre = _dot(hb, wq_ref[:, GDN_WIDTH:2 * GDN_WIDTH])
    _gdn_frontend(0, q_pre, cw_ref, q_ref, ngb_ref, gx, ns=ns, rs=tm // ns)
    v_pre = _dot(hb, wq_ref[:, 2 * GDN_WIDTH:])
    _gdn_frontend(1, k_pre, cw_ref, k_ref, ngb_ref, gx, ns=ns, rs=tm // ns)
    z_ref[...] = _dot(hb, wz_ref[...])
    _gdn_frontend(2, v_pre, cw_ref, v_ref, ngb_ref, gx, ns=ns, rs=tm // ns)
    ba_ref[...] = _dot(hb, wba_ref[...])


def _in_proj(h, g, cbuf, gbuf, wq, wz, wba, wu, wc, dw_w, cc_p, conv_w, tm, ns, tps):
    n = h.shape[0]
    row = lambda w: pl.BlockSpec((tm, w), lambda i: (i, 0))
    cc_blk = pl.BlockSpec((ns, CC_HIST, CC_WIDTH), lambda i: (i // tps, 0, 0))
    gdn_blk = pl.BlockSpec((ns, GDN_HIST, QKV_COLS), lambda i: (i // tps, 0, 0))
    nblk = n // (tm * tps)
    rs = tm // ns
    consts = [wq, wz, wba, wu, wc, dw_w, cc_p, conv_w]
    return pl.pallas_call(
        functools.partial(_inproj_kernel, tm=tm, ns=ns, tps=tps),
        grid=(n // tm,),
        in_specs=[row(D_MODEL), _const_spec((1, D_MODEL)), cc_blk, gdn_blk] + [_const_spec(a.shape) for a in consts],
        out_specs=[row(GDN_WIDTH), row(GDN_WIDTH), row(GDN_WIDTH), row(GDN_WIDTH), row(LANES),
                   _s5_block_spec(tm, tps), row(CC_WIDTH), cc_blk, gdn_blk],
        out_shape=[jax.ShapeDtypeStruct((n, GDN_WIDTH), F32)] * 4
        + [jax.ShapeDtypeStruct((n, LANES), F32),
           jax.ShapeDtypeStruct((nblk, S5_GROUPS, tps * tm // S5_CHUNK, S5_CHUNK * S5_GROUP), F32),
           jax.ShapeDtypeStruct((n, CC_WIDTH), F32), jax.ShapeDtypeStruct(cbuf.shape, F32),
           jax.ShapeDtypeStruct(gbuf.shape, F32)],
        scratch_shapes=[pltpu.VMEM((S5_WIDTH // LANES, tm, LANES), F32),
                        pltpu.VMEM((ns, rs + CC_PAD, CC_WIDTH), F32),
                        pltpu.VMEM((7, rs + CC_PAD - 8, CC_WIDTH), F32),
                        pltpu.VMEM((ns, rs + GDN_PAD, QKV_COLS), F32)],
        compiler_params=_cparams(("arbitrary",)),
        name="in_proj",
    )(h, g, cbuf, gbuf, *consts)


def _gdn_kernel(q_ref, k_ref, v_ref, z_ref, ba_ref, s0_ref, ad_ref, nw_ref,
                ya_ref, sn_ref,
                s_scr, gb_scr, *, tt, chunk):
    ti = pl.program_id(1)
    nt = pl.num_programs(1)
    hd = GDN_HEAD_DIM
    nh = GDN_HEADS
    c = chunk

    @pl.when(ti == 0)
    def _():
        s_scr[...] = s0_ref[0]

    ba = ba_ref[0]
    beta = jax.nn.sigmoid(ba)
    sp_in = ba + ad_ref[1:2, :]
    softplus = jnp.maximum(sp_in, 0.0) + jnp.log(1.0 + jnp.exp(-jnp.abs(sp_in)))
    g = -jnp.exp(ad_ref[0:1, :]) * softplus
    gb_scr[0] = beta
    gb_scr[1] = g

    nck = tt // c
    t_r = lax.broadcasted_iota(jnp.int32, (tt, tt), 0)
    t_c = lax.broadcasted_iota(jnp.int32, (tt, tt), 1)
    tril_bd = ((t_r // c == t_c // c) & (t_r >= t_c)).astype(BF16)
    pi = lax.broadcasted_iota(jnp.int32, (tt, nh * c), 0) % c
    pc = lax.broadcasted_iota(jnp.int32, (tt, nh * c), 1)
    ps = pc % c
    p_head = pc // c
    ci_r = lax.broadcasted_iota(jnp.int32, (c, nh * c), 0)
    ci_s = lax.broadcasted_iota(jnp.int32, (c, nh * c), 1) % c
    strict4 = ci_r > ci_s
    causal4 = ci_r >= ci_s
    eye4 = (ci_r == ci_s).astype(F32)
    bd_r = lax.broadcasted_iota(jnp.int32, (nh * c, nh * c), 0) // c
    bd_c = lax.broadcasted_iota(jnp.int32, (nh * c, nh * c), 1) // c
    bd_mask = (bd_r == bd_c).astype(BF16)
    kbd_r = lax.broadcasted_iota(jnp.int32, (nh * c, GDN_WIDTH), 0) // c
    kbd_c = lax.broadcasted_iota(jnp.int32, (nh * c, GDN_WIDTH), 1) // hd
    kbd_mask = (kbd_r == kbd_c).astype(BF16)
    n_dbl = int(math.log2(c)) - 1

    def blockdiag(xb):
        return jnp.concatenate([xb] * nh, axis=0) * bd_mask

    def mm_packed(a4, b4):
        return _dot(a4.astype(BF16), blockdiag(b4.astype(BF16)))

    def pad_rows(x, h):
        z = jnp.zeros_like(x)
        return jnp.concatenate([x if j == h else z for j in range(nh)], axis=0)

    g_all = gb_scr[1]
    gc_all = _dot_exact_lhs(tril_bd, g_all)
    g4 = jnp.zeros((tt, nh * c), F32)
    for h in range(nh):
        g4 = jnp.where(p_head == h, jnp.broadcast_to(g_all[:, nh + h:nh + h + 1], (tt, nh * c)), g4)
    decay_all = jnp.exp(_dot_exact_lhs(tril_bd, jnp.where(pi > ps, g4, 0.0)))

    chunk_rows = [slice(ck * c, (ck + 1) * c) for ck in range(nck)]
    g_last_b = jnp.concatenate(
        [jnp.broadcast_to(gc_all[(ck + 1) * c - 1:(ck + 1) * c], (c, LANES)) for ck in range(nck)], axis=0)
    e_gc = jnp.exp(gc_all)
    e_rem = jnp.exp(g_last_b - gc_all)
    e_last = [jnp.exp(gc_all[(ck + 1) * c - 1:(ck + 1) * c]) for ck in range(nck)]
    beta_all = gb_scr[0]
    q_all = q_ref[0]
    k_all = k_ref[0]
    v_all = v_ref[0]
    kb_parts, wq_b, kd_b, rhs_b = [], [], [], []
    for h in range(nh):
        sl = slice(h * hd, (h + 1) * hd)
        b_h = jnp.broadcast_to(beta_all[:, h:h + 1], (tt, hd))
        eg_h = jnp.broadcast_to(e_gc[:, nh + h:nh + h + 1], (tt, hd))
        er_h = jnp.broadcast_to(e_rem[:, nh + h:nh + h + 1], (tt, hd))
        kb_h = k_all[:, sl] * b_h
        kb_parts.append(kb_h)
        wq_b.append((q_all[:, sl] * eg_h).astype(BF16))
        kd_b.append((k_all[:, sl] * er_h).astype(BF16))
        rhs_b.append(jnp.concatenate([v_all[:, sl] * b_h, kb_h * eg_h], axis=1).astype(BF16))
    kb_b = jnp.concatenate(kb_parts, axis=1).astype(BF16)
    q_b = q_all.astype(BF16)
    k_b = k_all.astype(BF16)

    kq = []
    for rows in chunk_rows:
        k_bd = jnp.concatenate([k_b[rows]] * nh, axis=0) * kbd_mask
        kq.append(_dot_nt(jnp.concatenate([kb_b[rows], q_b[rows]], axis=0), k_bd))
    x4, t4, aqk4 = [], [], []
    for ck, rows in enumerate(chunk_rows):
        n4 = jnp.where(strict4, kq[ck][0:c] * decay_all[rows], 0.0)
        aqk4.append(jnp.where(causal4, kq[ck][c:2 * c] * decay_all[rows], 0.0).astype(BF16))
        x4.append(n4)
        t4.append(eye4 - n4)
    for _ in range(n_dbl):
        x4 = [mm_packed(x, x) for x in x4]
        t4 = [t + mm_packed(t, x) for t, x in zip(t4, x4)]
    uw = []
    for ck, rows in enumerate(chunk_rows):
        t4b = t4[ck].astype(BF16)
        uw.append([_dot(t4b, pad_rows(rhs_b[h][rows], h)) for h in range(nh)])

    s_cur = [s_scr[h] for h in range(nh)]
    for ck, rows in enumerate(chunk_rows):
        ws = [_dot(jnp.concatenate([uw[ck][h][:, hd:2 * hd].astype(BF16), wq_b[h][rows]], axis=0),
                   s_cur[h].astype(BF16)) for h in range(nh)]
        v_new = [(uw[ck][h][:, 0:hd] - ws[h][0:c]).astype(BF16) for h in range(nh)]
        s_add = [_dot_tn(kd_b[h][rows], v_new[h]) for h in range(nh)]
        o_att = [_dot(aqk4[ck], pad_rows(v_new[h], h)) for h in range(nh)]
        for h in range(nh):
            sl = slice(h * hd, (h + 1) * hd)
            dl = jnp.broadcast_to(e_last[ck][:, nh + h:nh + h + 1], (hd, hd))
            s_cur[h] = s_cur[h] * dl + s_add[h]
            o_h = ws[h][c:2 * c] + o_att[h]
            o_n = o_h * lax.rsqrt(jnp.mean(o_h * o_h, axis=-1, keepdims=True) + 1e-6) * nw_ref[...]
            ya_ref[0, rows, sl] = o_n * _silu(z_ref[0, rows, sl])
    for h in range(nh):
        s_scr[h] = s_cur[h]

    @pl.when(ti == nt - 1)
    def _():
        sn_ref[0] = s_scr[...]


def _gdn(q, k, v, z, ba, s0, ad, norm_w, tt, chunk):
    b, t, _ = q.shape
    kern = functools.partial(_gdn_kernel, tt=tt, chunk=chunk)
    tile = lambda w: pl.BlockSpec((1, tt, w), lambda i, j: (i, j, 0))
    state = pl.BlockSpec((1, GDN_HEADS, GDN_HEAD_DIM, GDN_HEAD_DIM), lambda i, j: (i, 0, 0, 0))
    return pl.pallas_call(
        kern,
        grid=(b, t // tt),
        in_specs=[tile(GDN_WIDTH), tile(GDN_WIDTH), tile(GDN_WIDTH), tile(GDN_WIDTH), tile(LANES), state,
                  _const_spec(ad.shape), _const_spec(norm_w.shape)],
        out_specs=[tile(GDN_WIDTH), state],
        out_shape=[jax.ShapeDtypeStruct((b, t, GDN_WIDTH), F32),
                   jax.ShapeDtypeStruct((b, GDN_HEADS, GDN_HEAD_DIM, GDN_HEAD_DIM), F32)],
        scratch_shapes=[pltpu.VMEM((GDN_HEADS, GDN_HEAD_DIM, GDN_HEAD_DIM), F32),
                        pltpu.VMEM((2, tt, LANES), F32)],
        compiler_params=_cparams(("parallel", "arbitrary")),
        name="gdn",
    )(q, k, v, z, ba, s0, ad, norm_w)


def _s5_prep_kernel(lam_ref, ldt_ref, bt_ref, cm_ref, ktoep_ref, bend_ref, cst_ref, lamc_ref):
    g_n, p_n, lc = S5_GROUPS, S5_STATE, S5_CHUNK
    lr, li = lam_ref[0], lam_ref[1]
    dt = jnp.exp(ldt_ref[...])
    mag = jnp.exp(lr * dt)
    ang = li * dt
    br, bi = mag * jnp.cos(ang), mag * jnp.sin(ang)
    den = lr * lr + li * li
    qr = ((br - 1.0) * lr + bi * li) / den
    qi = (bi * lr - (br - 1.0) * li) / den
    bbr = qr * bt_ref[0] - qi * bt_ref[1]
    bbi = qr * bt_ref[1] + qi * bt_ref[0]
    cr, cim = cm_ref[0], cm_ref[1]

    pw = [(jnp.ones_like(br), jnp.zeros_like(br))]
    for _ in range(lc):
        ar, ai = pw[-1]
        pw.append((ar * br - ai * bi, ar * bi + ai * br))
    lamc_ref[:, :, 0:p_n] = pw[lc][0]
    lamc_ref[:, :, p_n:2 * p_n] = pw[lc][1]

    for s in range(lc):
        ar, ai = pw[lc - 1 - s]
        bend_ref[:, s, :, 0:p_n] = ar * bbr - ai * bbi
        bend_ref[:, s, :, p_n:2 * p_n] = ar * bbi + ai * bbr
        ar, ai = pw[s + 1]
        cst_ref[:, s, :, 0:p_n] = ar * cr - ai * cim
        cst_ref[:, s, :, p_n:2 * p_n] = -(ar * cim + ai * cr)

    lane = lax.broadcasted_iota(jnp.int32, (S5_GROUP, lc * S5_GROUP), 1)
    for g in range(g_n):
        cpr = jnp.concatenate([pw[j][0][g] * cr[g] - pw[j][1][g] * cim[g] for j in range(lc)], axis=0)
        cpi = jnp.concatenate([pw[j][0][g] * cim[g] + pw[j][1][g] * cr[g] for j in range(lc)], axis=0)
        bh_r, bl_r = _split2(bbr[g])
        bh_i, bl_i = _split2(bbi[g])
        ch_r, cl_r = _split2(cpr)
        ch_i, cl_i = _split2(cpi)
        krow = (_dot_nt(bh_r, ch_r) + (_dot_nt(bh_r, cl_r) + _dot_nt(bl_r, ch_r))
                - (_dot_nt(bh_i, ch_i) + (_dot_nt(bh_i, cl_i) + _dot_nt(bl_i, ch_i))))
        for s in range(lc):
            sh = s * S5_GROUP
            blk = krow if s == 0 else jnp.where(lane >= sh, pltpu.roll(krow, sh, 1), 0.0)
            ktoep_ref[g, s * S5_GROUP:(s + 1) * S5_GROUP, :] = blk.astype(BF16)


def _s5_prep(lam, ldt, bt, cm):
    g_n, p_n, lc = S5_GROUPS, S5_STATE, S5_CHUNK
    w = lc * S5_GROUP
    return pl.pallas_call(
        _s5_prep_kernel,
        out_shape=[jax.ShapeDtypeStruct((g_n, w, w), BF16),
                   jax.ShapeDtypeStruct((g_n, lc, S5_GROUP, 2 * p_n), F32),
                   jax.ShapeDtypeStruct((g_n, lc, S5_GROUP, 2 * p_n), F32),
                   jax.ShapeDtypeStruct((g_n, 1, 2 * p_n), F32)],
        compiler_params=pltpu.CompilerParams(vmem_limit_bytes=VMEM_LIMIT),
        name="s5_prep",
    )(lam, ldt, bt, cm)


def _s5_kernel(u_ref, h0_ref, ktoep_ref, bend_ref, cst_ref, lamc_ref, d_ref, y_ref, hn_ref, *, n, nb):
    p_n = S5_STATE
    r_n = nb * n
    lane = lax.broadcasted_iota(jnp.int32, (1, 2 * p_n), 1)
    sign = jnp.where(lane < p_n, -1.0, 1.0)
    rowm = lax.broadcasted_iota(jnp.int32, (r_n, 2 * p_n), 0) % n
    if nb > 1:
        e_r = lax.broadcasted_iota(jnp.int32, (r_n, nb), 0)
        e_b = lax.broadcasted_iota(jnp.int32, (r_n, nb), 1)
        expand = (e_r // n == e_b).astype(BF16)
        s_b = lax.broadcasted_iota(jnp.int32, (nb, r_n), 0)
        s_r = lax.broadcasted_iota(jnp.int32, (nb, r_n), 1)
        pick_last = (s_r == s_b * n + (n - 1)).astype(BF16)

    def lam_tiles(g):
        lam = lamc_ref[pl.ds(g, 1), :]
        lam_sw = pltpu.roll(lam, p_n, 1)
        return (jnp.where(lane < p_n, lam, lam_sw),
                jnp.where(lane < p_n, lam_sw, lam) * sign)

    def cmul(x, a_rr, a_is):
        return x * a_rr + pltpu.roll(x, p_n, 1) * a_is

    def groups_body(gi, carry):
        gs = [gi * S5_UNROLL + j for j in range(S5_UNROLL)]
        us = [u_ref[0, g] for g in gs]
        lams = [lam_tiles(g) for g in gs]
        x_loc = [_dot_x3(u, bend_ref[g]) for u, g in zip(us, gs)]
        xs = []
        for g, xl in zip(gs, x_loc):
            h0 = h0_ref[0, g]
            h0_rows = _dot_exact_lhs(expand, h0) if nb > 1 else h0
            xs.append(jnp.where(rowm == 0, h0_rows, pltpu.roll(xl, 1, 0)))
        pw = list(lams)
        step = 1
        while step < n:
            xs = [x + cmul(jnp.where(rowm >= step, pltpu.roll(x, step, 0), 0.0), a_rr, a_is)
                  for x, (a_rr, a_is) in zip(xs, pw)]
            pw = [(a_rr * a_rr - a_is * a_is, 2.0 * a_rr * a_is) for a_rr, a_is in pw]
            step *= 2
        for g, u, x, xl, (b_rr, b_is) in zip(gs, us, xs, x_loc, lams):
            y = _dot(u.astype(BF16), ktoep_ref[g]) + _dot_nt(x.astype(BF16), cst_ref[g].astype(BF16))
            y = y + d_ref[pl.ds(g, 1), :] * u
            y_ref[0, g] = jax.nn.gelu(y)
            x_end = cmul(x, b_rr, b_is) + xl
            hn_ref[0, g] = _dot_exact_lhs(pick_last, x_end) if nb > 1 else x_end[n - 1:n, :]
        return carry

    lax.fori_loop(0, S5_GROUPS // S5_UNROLL, groups_body, 0)


def _s5(u, h0, ktoep, bend, cst, lamc, d_t, n):
    nblk, g_n, r_n, w = u.shape
    nb = r_n // n
    p2 = 2 * S5_STATE
    kern = functools.partial(_s5_kernel, n=n, nb=nb)
    return pl.pallas_call(
        kern,
        grid=(nblk,),
        in_specs=[pl.BlockSpec((1, g_n, r_n, w), lambda i: (i, 0, 0, 0)),
                  pl.BlockSpec((1, g_n, nb, p2), lambda i: (i, 0, 0, 0)),
                  _const_spec(ktoep.shape), _const_spec(bend.shape), _const_spec(cst.shape),
                  _const_spec(lamc.shape), _const_spec(d_t.shape)],
        out_specs=[pl.BlockSpec((1, g_n, r_n, w), lambda i: (i, 0, 0, 0)),
                   pl.BlockSpec((1, g_n, nb, p2), lambda i: (i, 0, 0, 0))],
        out_shape=[jax.ShapeDtypeStruct((nblk, g_n, r_n, w), F32),
                   jax.ShapeDtypeStruct((nblk, g_n, nb, p2), F32)],
        compiler_params=_cparams(("parallel",)),
        name="s5",
    )(u, h0, ktoep, bend, cst, lamc, d_t)


CC_HIST = CC_KERNEL - 1
CC_PAD = 32


def _cc_load_history(buf_ref, xbuf, *, ns, rs, first_tile):
    lo = CC_PAD - CC_HIST
    if ns == 1:
        @pl.when(first_tile)
        def _():
            xbuf[0, lo:CC_PAD, :] = buf_ref[0]

        @pl.when(jnp.logical_not(first_tile))
        def _():
            xbuf[0, lo:CC_PAD, :] = xbuf[0, rs + lo:rs + CC_PAD, :]
    else:
        for q in range(ns):
            xbuf[q, lo:CC_PAD, :] = buf_ref[q]


def _cc_module(cc, w_ref, p_ref, nb_ref, xbuf, sh_scr, *, ns, rs):
    x = cc[:, 0:CC_WIDTH] * jax.nn.sigmoid(cc[:, CC_WIDTH:])
    lo = CC_PAD - CC_HIST
    outs = []
    for q in range(ns):
        xbuf[q, CC_PAD:CC_PAD + rs, :] = x[q * rs:(q + 1) * rs]
        nb_ref[q] = xbuf[q, rs + lo:rs + CC_PAD, :]
        for a in range(1, 8):
            sh_scr[a - 1] = xbuf[q, a:a + rs + CC_PAD - 8, :]
        acc = None
        for j in range(CC_KERNEL):
            a, m = (lo + j) % 8, (lo + j) // 8
            win = sh_scr[a - 1, 8 * m:8 * m + rs, :] if a else xbuf[q, 8 * m:8 * m + rs, :]
            term = w_ref[j:j + 1, :] * win
            acc = term if acc is None else acc + term
        outs.append(acc)
    acc = (outs[0] if ns == 1 else jnp.concatenate(outs, axis=0)) + p_ref[0:1, :]
    mu = jnp.mean(acc, axis=-1, keepdims=True)
    xc = acc - mu
    var = jnp.mean(xc * xc, axis=-1, keepdims=True)
    return _silu(xc * lax.rsqrt(var + 1e-5) * p_ref[1:2, :] + p_ref[2:3, :])


FFN_SPLITS = ((0, 1024), (1024, 2048), (2048, FFN_HIDDEN))


def _out_ffn_kernel(h_ref, ya_ref, yf_ref, yc_ref, p_ref, gw_ref, gb_ref,
                    woa_ref, wob_ref, woc_ref, nf_ref, w1_ref, w3_ref, w2_ref, pew_ref, peg_ref, nfin_ref,
                    o_ref, yb_scr, ys_scr, *, tm, final):
    nck = tm // S5_CHUNK
    gpl = LANES // S5_GROUP
    for s in range(S5_CHUNK):
        for g in range(S5_GROUPS):
            ys_scr[s, g // gpl, :, (g % gpl) * S5_GROUP:(g % gpl + 1) * S5_GROUP] = (
                yf_ref[0, g, :, s * S5_GROUP:(s + 1) * S5_GROUP])
        for half in range(S5_WIDTH // LANES):
            yb_scr[half, pl.ds(s, nck, stride=S5_CHUNK), :] = ys_scr[s, half]
    yb = jnp.concatenate([yb_scr[half] for half in range(S5_WIDTH // LANES)], axis=1)
    yb = yb * jax.nn.sigmoid(_dot(yb.astype(BF16), gw_ref[...]) + gb_ref[...])
    h = h_ref[...] + (_dot(ya_ref[...].astype(BF16), woa_ref[...])
                      + _dot(yb.astype(BF16), wob_ref[...])
                      + _dot(yc_ref[...].astype(BF16), woc_ref[...]))
    ms = jnp.mean(h * h, axis=-1, keepdims=True)
    hf = ((h * lax.rsqrt(ms + 1e-6)) * nf_ref[...]).astype(BF16)
    for lo, hi in FFN_SPLITS:
        a = _dot(hf, w1_ref[:, lo:hi])
        b = _dot(hf, w3_ref[:, lo:hi])
        h = h + _dot((_silu(a) * b).astype(BF16), w2_ref[lo:hi, :])
    pe = _dot(p_ref[...].astype(BF16), pew_ref[...])
    h = h + pe * jax.nn.sigmoid(_dot(h.astype(BF16), peg_ref[...]))
    if final:
        ms = jnp.mean(h * h, axis=-1, keepdims=True)
        h = (h * lax.rsqrt(ms + 1e-6)) * nfin_ref[...]
    o_ref[...] = h


def _out_ffn(h, ya, yf, yc, p, consts, tm, tps, final):
    n = h.shape[0]
    row = lambda w: pl.BlockSpec((tm, w), lambda i: (i, 0))
    return pl.pallas_call(
        functools.partial(_out_ffn_kernel, tm=tm, final=final),
        grid=(n // tm,),
        in_specs=[row(D_MODEL), row(GDN_WIDTH), _s5_block_spec(tm, tps), row(CC_WIDTH), row(PLE_DIM)]
        + [_const_spec(a.shape) for a in consts],
        out_specs=row(D_MODEL),
        out_shape=jax.ShapeDtypeStruct((n, D_MODEL), F32),
        scratch_shapes=[pltpu.VMEM((S5_WIDTH // LANES, tm, LANES), F32),
                        pltpu.VMEM((S5_CHUNK, S5_WIDTH // LANES, tm // S5_CHUNK, LANES), F32)],
        compiler_params=_cparams(("parallel",)),
        name="out_ffn",
    )(h, ya, yf, yc, p, *consts)


def _tile_rows(n, want):
    t = min(n, want)
    while n % t:
        t //= 2
    return t


def _prep_layer(i, W):
    w_in = W['w_in'][i]
    L = {}
    L['g_mix'] = W['norm_mix'][i][None, :]
    L['wq'] = w_in[:, :OFF_Z].astype(BF16)
    L['wz'] = w_in[:, OFF_Z:OFF_BA].astype(BF16)
    L['wba'] = jnp.pad(w_in[:, OFF_BA:OFF_S5], ((0, 0), (0, LANES - 2 * GDN_HEADS))).astype(BF16)
    L['wu'] = w_in[:, OFF_S5:OFF_CC].astype(BF16)
    L['wc'] = w_in[:, OFF_CC:].astype(BF16)
    L['conv_w'] = W['gdn_conv_w'][i]
    padl = (GDN_HEADS, LANES - 2 * GDN_HEADS)
    L['ad'] = jnp.stack([jnp.pad(W['gdn_a_log'][i], padl), jnp.pad(W['gdn_dt_bias'][i], padl)])
    L['gdn_norm'] = W['gdn_norm'][i][None, :]
    lam = jnp.stack([W['s5_lam_re'][i], W['s5_lam_im'][i]])[:, :, None, :]
    bt =jnp.stack([jnp.swapaxes(W['s5_b_re'][i], 1, 2), jnp.swapaxes(W['s5_b_im'][i], 1, 2)])
    cm = jnp.stack([W['s5_c_re'][i], W['s5_c_im'][i]])
    ktoep, bend, cst, lamc = _s5_prep(lam, W['s5_log_dt'][i][:, None, None], bt, cm)
    w = S5_CHUNK * S5_GROUP
    L['ktoep'] = ktoep
    L['bend'] = bend.reshape(S5_GROUPS, w, 2 * S5_STATE)
    L['cst'] = cst.reshape(S5_GROUPS, w, 2 * S5_STATE)
    L['lamc'] = lamc.reshape(S5_GROUPS, 2 * S5_STATE)
    L['d_t'] = jnp.tile(W['s5_d'][i].reshape(S5_GROUPS, S5_GROUP), (1, S5_CHUNK))
    L['glu_w'] = W['s5_glu_w'][i].astype(BF16)
    L['glu_b'] = W['s5_glu_b'][i][None, :]
    L['dw_w'] = W['cc_dw_w'][i]
    L['cc_p'] = jnp.stack([W['cc_dw_b'][i], W['cc_ln_g'][i], W['cc_ln_b'][i]])
    wo = W['w_out'][i].astype(BF16)
    L['woa'], L['wob'], L['woc'] = wo[:GDN_WIDTH], wo[GDN_WIDTH:GDN_WIDTH + S5_WIDTH], wo[GDN_WIDTH + S5_WIDTH:]
    L['g_ffn'] = W['norm_ffn'][i][None, :]
    L['w1'] = W['ffn_w1'][i].astype(BF16)
    L['w3'] = W['ffn_w3'][i].astype(BF16)
    L['w2'] = W['ffn_w2'][i].astype(BF16)
    L['pew'] = W['pe_w'][i].astype(BF16)
    L['peg'] = W['pe_gate_w'][i].astype(BF16)
    L['g_fin'] = W['norm_final'][None, :]
    return L


def _layer(h, p, st_gdn, st_gconv, st_s5, st_conv, L, final):
    b, t, _ = h.shape
    n = b * t
    tm = _tile_rows(n, ROW_TILE)
    ns, tps = _seq_tiling(b, t, tm)
    q, k, v, z, ba, u_f, yc, b_conv, b_gconv = _in_proj(
        h.reshape(n, D_MODEL), L['g_mix'], st_conv, st_gconv, L['wq'], L['wz'], L['wba'], L['wu'], L['wc'],
        L['dw_w'], L['cc_p'], L['conv_w'], tm, ns, tps)

    chunk = 64 if t % 64 == 0 else t
    tt = _tile_rows(t, 256)
    seq = lambda a: a.reshape(b, t, a.shape[-1])
    ya, s_gdn = _gdn(seq(q), seq(k), seq(v), seq(z), seq(ba), st_gdn, L['ad'], L['gdn_norm'], tt, chunk)

    h0 = jnp.concatenate([st_s5[..., 0], st_s5[..., 1]], axis=-1)
    h0 = jnp.swapaxes(h0.reshape(b // ns, ns, S5_GROUPS, 2 * S5_STATE), 1, 2)
    y_f, hn = _s5(u_f, h0, L['ktoep'], L['bend'], L['cst'], L['lamc'], L['d_t'], t // S5_CHUNK)
    hn = jnp.swapaxes(hn, 1, 2).reshape(b, S5_GROUPS, 2 * S5_STATE)
    s_s5 = jnp.stack([hn[..., :S5_STATE], hn[..., S5_STATE:]], axis=-1)

    consts = [L['glu_w'], L['glu_b'], L['woa'], L['wob'], L['woc'], L['g_ffn'],
              L['w1'], L['w3'], L['w2'], L['pew'], L['peg'], L['g_fin']]
    h_new = _out_ffn(h.reshape(n, D_MODEL), ya.reshape(n, GDN_WIDTH), y_f, yc, p.reshape(n, PLE_DIM),
                     consts, tm, tps, final)
    return h_new.reshape(b, t, D_MODEL), s_gdn, b_gconv, s_s5, b_conv


def _trunk(x, p, st_gdn, st_gconv, st_s5, st_conv, layers):
    h = x
    outs = ([], [], [], [])
    depth = len(layers)
    for i, L in enumerate(layers):
        h, *st = _layer(h, p[i], st_gdn[i], st_gconv[i], st_s5[i], st_conv[i], L, i == depth - 1)
        for acc, s in zip(outs, st):
            acc.append(s)
    return (h,) + tuple(jnp.stack(o) for o in outs)


def kernel(x_prompt, x_sample, p_prompt, p_sample, state_gdn, state_gdn_conv, state_s5, state_conv, norm_mix, w_in, gdn_conv_w, gdn_a_log, gdn_dt_bias, gdn_norm, s5_lam_re, s5_lam_im, s5_log_dt, s5_b_re, s5_b_im, s5_c_re, s5_c_im, s5_d, s5_glu_w, s5_glu_b, cc_dw_w, cc_dw_b, cc_ln_g, cc_ln_b, w_out, norm_ffn, ffn_w1, ffn_w3, ffn_w2, pe_w, pe_gate_w, norm_final):
    W = dict(norm_mix=norm_mix, w_in=w_in, gdn_conv_w=gdn_conv_w, gdn_a_log=gdn_a_log, gdn_dt_bias=gdn_dt_bias,
             gdn_norm=gdn_norm, s5_lam_re=s5_lam_re, s5_lam_im=s5_lam_im, s5_log_dt=s5_log_dt, s5_b_re=s5_b_re,
             s5_b_im=s5_b_im, s5_c_re=s5_c_re, s5_c_im=s5_c_im, s5_d=s5_d, s5_glu_w=s5_glu_w, s5_glu_b=s5_glu_b,
             cc_dw_w=cc_dw_w, cc_dw_b=cc_dw_b, cc_ln_g=cc_ln_g, cc_ln_b=cc_ln_b, w_out=w_out, norm_ffn=norm_ffn,
             ffn_w1=ffn_w1, ffn_w3=ffn_w3, ffn_w2=ffn_w2, pe_w=pe_w, pe_gate_w=pe_gate_w, norm_final=norm_final)
    depth = w_in.shape[0]
    layers = [_prep_layer(i, W) for i in range(depth)]
    bp = x_prompt.shape[0]
    z_gdn = jnp.zeros((depth, bp, GDN_HEADS, GDN_HEAD_DIM, GDN_HEAD_DIM), F32)
    z_gconv = jnp.zeros((depth, bp, GDN_CONV - 1, QKV_COLS), F32)
    z_s5 = jnp.zeros((depth, bp, S5_GROUPS, S5_STATE, 2), F32)
    z_conv = jnp.zeros((depth, bp, CC_KERNEL - 1, CC_WIDTH), F32)
    y_p, gdn_p, gconv_p, s5_p, conv_p = _trunk(x_prompt, p_prompt, z_gdn, z_gconv, z_s5, z_conv, layers)
    y_s, gdn_s, gconv_s, s5_s, conv_s = _trunk(x_sample, p_sample, state_gdn, state_gdn_conv, state_s5,
                                               state_conv, layers)
    return (y_p, y_s, gdn_p, gconv_p, s5_p, conv_p, gdn_s, gconv_s, s5_s, conv_s)
```

```python
import functools
import math

import jax
import jax.numpy as jnp
from jax import lax
from jax.experimental import pallas as pl
from jax.experimental.pallas import tpu as pltpu

F32 = jnp.float32
BF16 = jnp.bfloat16

D_MODEL = 1024
GDN_HEADS = 4
GDN_HEAD_DIM = 128
GDN_WIDTH = GDN_HEADS * GDN_HEAD_DIM
GDN_CONV = 4
QKV_COLS = 3 * GDN_WIDTH
S5_WIDTH = 256
S5_GROUP = 16
S5_GROUPS = 16
S5_STATE = 64
S5_CHUNK = 16
S5_UNROLL = 4
CC_WIDTH = 256
CC_KERNEL = 31
FFN_HIDDEN = 2816
ROW_TILE = 512
PLE_DIM = 256
OFF_Z = QKV_COLS
OFF_BA = OFF_Z + GDN_WIDTH
OFF_S5 = OFF_BA + 2 * GDN_HEADS
OFF_CC = OFF_S5 + S5_WIDTH
IN_COLS = OFF_CC + 2 * CC_WIDTH
LANES = 128
VMEM_LIMIT = 56 * 1024 * 1024


def _cparams(sem):
    return pltpu.CompilerParams(dimension_semantics=sem, vmem_limit_bytes=VMEM_LIMIT)


def _const_spec(shape):
    nd = len(shape)
    return pl.BlockSpec(shape, lambda *_: (0,) * nd, pipeline_mode=pl.Buffered(1))


def _split2(x):
    hi = x.astype(BF16)
    lo = (x - hi.astype(F32)).astype(BF16)
    return hi, lo


def _split3(x):
    hi = x.astype(BF16)
    r = x - hi.astype(F32)
    mid = r.astype(BF16)
    lo = (r - mid.astype(F32)).astype(BF16)
    return hi, mid, lo


def _dot(a, b):
    return jnp.dot(a, b, preferred_element_type=F32)


def _dot_nt(a, b):
    return lax.dot_general(a, b, (((1,), (1,)), ((), ())), preferred_element_type=F32)


def _dot_tn(a, b):
    return lax.dot_general(a, b, (((0,), (0,)), ((), ())), preferred_element_type=F32)


def _dot_x3(a, b):
    ah, al = _split2(a)
    bh, bl = _split2(b)
    return _dot(ah, bh) + (_dot(ah, bl) + _dot(al, bh))


def _dot_exact_lhs(l_bf16, x):
    h, m, lo = _split3(x)
    return _dot(l_bf16, h) + (_dot(l_bf16, m) + _dot(l_bf16, lo))


def _silu(x):
    return x * jax.nn.sigmoid(x)


def _seq_tiling(b, t, tm):
    if t >= tm:
        assert t % tm == 0
        return 1, t // tm
    assert tm % t == 0 and (b * t) % tm == 0
    return tm // t, 1


def _s5_block_spec(tm, tps):
    return pl.BlockSpec((1, S5_GROUPS, tm // S5_CHUNK, S5_CHUNK * S5_GROUP), lambda i: (i // tps, 0, i % tps, 0))


def _inproj_kernel(h_ref, g_ref, cbuf_ref, wq_ref, wz_ref, wba_ref, wu_ref, wc_ref, dw_ref, ccp_ref,
                   qkv_ref, z_ref, ba_ref, uf_ref, yc_ref, nb_ref, u_scr, xbuf, sh_scr, *, tm, ns, tps):
    _cc_load_history(cbuf_ref, xbuf, ns=ns, rs=tm // ns, first_tile=pl.program_id(0) % tps == 0)
    x = h_ref[...]
    ms = jnp.mean(x * x, axis=-1, keepdims=True)
    hn = (x * lax.rsqrt(ms + 1e-6)) * g_ref[...]
    hb = hn.astype(BF16)
    u = _dot(hb, wu_ref[...])
    nck = tm // S5_CHUNK
    gpl = LANES // S5_GROUP
    for half in range(S5_WIDTH // LANES):
        u_scr[half] = u[:, half * LANES:(half + 1) * LANES]
    cc = _dot(hb, wc_ref[...])
    for s in range(S5_CHUNK):
        for half in range(S5_WIDTH // LANES):
            rows = u_scr[half, pl.ds(s, nck, stride=S5_CHUNK), :]
            for g in range(gpl):
                uf_ref[0, half * gpl + g, :, s * S5_GROUP:(s + 1) * S5_GROUP] = (
                    rows[:, g * S5_GROUP:(g + 1) * S5_GROUP])
    qkv_ref[...] = _dot(hb, wq_ref[...])
    z_ref[...] = _dot(hb, wz_ref[...])
    ba_ref[...] = _dot(hb, wba_ref[...])
    yc_ref[...] = _cc_module(cc, dw_ref, ccp_ref, nb_ref, xbuf, sh_scr, ns=ns, rs=tm // ns)


def _in_proj(h, g, cbuf, wq, wz, wba, wu, wc, dw_w, cc_p, tm, ns, tps):
    n = h.shape[0]
    row = lambda w: pl.BlockSpec((tm, w), lambda i: (i, 0))
    seq_blk = pl.BlockSpec((ns, CC_HIST, CC_WIDTH), lambda i: (i // tps, 0, 0))
    nblk = n // (tm * tps)
    rs = tm // ns
    consts = [wq, wz, wba, wu, wc, dw_w, cc_p]
    return pl.pallas_call(
        functools.partial(_inproj_kernel, tm=tm, ns=ns, tps=tps),
        grid=(n // tm,),
        in_specs=[row(D_MODEL), _const_spec((1, D_MODEL)), seq_blk] + [_const_spec(a.shape) for a in consts],
        out_specs=[row(QKV_COLS), row(GDN_WIDTH), row(LANES), _s5_block_spec(tm, tps), row(CC_WIDTH), seq_blk],
        out_shape=[jax.ShapeDtypeStruct((n, QKV_COLS), F32), jax.ShapeDtypeStruct((n, GDN_WIDTH), F32),
                   jax.ShapeDtypeStruct((n, LANES), F32),
                   jax.ShapeDtypeStruct((nblk, S5_GROUPS, tps * tm // S5_CHUNK, S5_CHUNK * S5_GROUP), F32),
                   jax.ShapeDtypeStruct((n, CC_WIDTH), F32), jax.ShapeDtypeStruct(cbuf.shape, F32)],
        scratch_shapes=[pltpu.VMEM((S5_WIDTH // LANES, tm, LANES), F32),
                        pltpu.VMEM((ns, rs + CC_PAD, CC_WIDTH), F32),
                        pltpu.VMEM((7, rs + CC_PAD - 8, CC_WIDTH), F32)],
        compiler_params=_cparams(("arbitrary",)),
        name="in_proj",
    )(h, g, cbuf, *consts)


def _gdn_kernel(qkv_ref, z_ref, ba_ref, buf_ref, s0_ref, cw_ref, ad_ref, nw_ref,
                ya_ref, sn_ref, nb_ref,
                xbuf, s_scr, q_scr, k_scr, v_scr, gb_scr, *, tt, chunk):
    ti = pl.program_id(1)
    nt = pl.num_programs(1)
    hd = GDN_HEAD_DIM
    nh = GDN_HEADS
    c = chunk
    pad = 8
    hist = GDN_CONV - 1

    @pl.when(ti == 0)
    def _():
        xbuf[pad - hist:pad, :] = buf_ref[0]
        s_scr[...] = s0_ref[0]

    @pl.when(ti > 0)
    def _():
        xbuf[pad - hist:pad, :] = xbuf[pad + tt - hist:pad + tt, :]

    xbuf[pad:pad + tt, :] = qkv_ref[0]
    nb_ref[0] = xbuf[pad + tt - hist:pad + tt, :]

    for cb in range(3 * nh):
        cols = slice(cb * hd, (cb + 1) * hd)
        acc = cw_ref[0:1, cols] * xbuf[pad - hist:pad - hist + tt, cols]
        for j in range(1, GDN_CONV):
            acc = acc + cw_ref[j:j + 1, cols] * xbuf[pad - hist + j:pad - hist + j + tt, cols]
        y = _silu(acc)
        h = cb % nh
        if cb < nh:
            q_scr[:, h * hd:(h + 1) * hd] = y * (lax.rsqrt(jnp.sum(y * y, axis=-1, keepdims=True) + 1e-6)
                                                 * (hd ** -0.5))
        elif cb < 2 * nh:
            k_scr[:, h * hd:(h + 1) * hd] = y * lax.rsqrt(jnp.sum(y * y, axis=-1, keepdims=True) + 1e-6)
        else:
            v_scr[:, h * hd:(h + 1) * hd] = y

    ba = ba_ref[0]
    beta = jax.nn.sigmoid(ba)
    sp_in = ba + ad_ref[1:2, :]
    softplus = jnp.maximum(sp_in, 0.0) + jnp.log(1.0 + jnp.exp(-jnp.abs(sp_in)))
    g = -jnp.exp(ad_ref[0:1, :]) * softplus
    gb_scr[0] = beta
    gb_scr[1] = g

    nck = tt // c
    t_r = lax.broadcasted_iota(jnp.int32, (tt, tt), 0)
    t_c = lax.broadcasted_iota(jnp.int32, (tt, tt), 1)
    tril_bd = ((t_r // c == t_c // c) & (t_r >= t_c)).astype(BF16)
    pi = lax.broadcasted_iota(jnp.int32, (tt, nh * c), 0) % c
    pc = lax.broadcasted_iota(jnp.int32, (tt, nh * c), 1)
    ps = pc % c
    p_head = pc // c
    ci_r = lax.broadcasted_iota(jnp.int32, (c, nh * c), 0)
    ci_s = lax.broadcasted_iota(jnp.int32, (c, nh * c), 1) % c
    strict4 = ci_r > ci_s
    causal4 = ci_r >= ci_s
    eye4 = (ci_r == ci_s).astype(F32)
    bd_r = lax.broadcasted_iota(jnp.int32, (nh * c, nh * c), 0) // c
    bd_c = lax.broadcasted_iota(jnp.int32, (nh * c, nh * c), 1) // c
    bd_mask = (bd_r == bd_c).astype(BF16)
    kbd_r = lax.broadcasted_iota(jnp.int32, (nh * c, GDN_WIDTH), 0) // c
    kbd_c = lax.broadcasted_iota(jnp.int32, (nh * c, GDN_WIDTH), 1) // hd
    kbd_mask = (kbd_r == kbd_c).astype(BF16)
    n_dbl = int(math.log2(c)) - 1

    def blockdiag(xb):
        return jnp.concatenate([xb] * nh, axis=0) * bd_mask

    def mm_packed(a4, b4):
        return _dot(a4.astype(BF16), blockdiag(b4.astype(BF16)))

    def pad_rows(x, h):
        z = jnp.zeros_like(x)
        return jnp.concatenate([x if j == h else z for j in range(nh)], axis=0)

    g_all = gb_scr[1]
    gc_all = _dot_exact_lhs(tril_bd, g_all)
    g4 = jnp.zeros((tt, nh * c), F32)
    for h in range(nh):
        g4 = jnp.where(p_head == h, jnp.broadcast_to(g_all[:, nh + h:nh + h + 1], (tt, nh * c)), g4)
    decay_all = jnp.exp(_dot_exact_lhs(tril_bd, jnp.where(pi > ps, g4, 0.0)))

    chunk_rows = [slice(ck * c, (ck + 1) * c) for ck in range(nck)]
    g_last_b = jnp.concatenate(
        [jnp.broadcast_to(gc_all[(ck + 1) * c - 1:(ck + 1) * c], (c, LANES)) for ck in range(nck)], axis=0)
    e_gc = jnp.exp(gc_all)
    e_rem = jnp.exp(g_last_b - gc_all)
    e_last = [jnp.exp(gc_all[(ck + 1) * c - 1:(ck + 1) * c]) for ck in range(nck)]
    beta_all = gb_scr[0]
    q_all = q_scr[...]
    k_all = k_scr[...]
    v_all = v_scr[...]
    kb_parts, wq_b, kd_b, rhs_b = [], [], [], []
    for h in range(nh):
        sl = slice(h * hd, (h + 1) * hd)
        b_h = jnp.broadcast_to(beta_all[:, h:h + 1], (tt, hd))
        eg_h = jnp.broadcast_to(e_gc[:, nh + h:nh + h + 1], (tt, hd))
        er_h = jnp.broadcast_to(e_rem[:, nh + h:nh + h + 1], (tt, hd))
        kb_h = k_all[:, sl] * b_h
        kb_parts.append(kb_h)
        wq_b.append((q_all[:, sl] * eg_h).astype(BF16))
        kd_b.append((k_all[:, sl] * er_h).astype(BF16))
        rhs_b.append(jnp.concatenate([v_all[:, sl] * b_h, kb_h * eg_h], axis=1).astype(BF16))
    kb_b = jnp.concatenate(kb_parts, axis=1).astype(BF16)
    q_b = q_all.astype(BF16)
    k_b = k_all.astype(BF16)

    kq = []
    for rows in chunk_rows:
        k_bd = jnp.concatenate([k_b[rows]] * nh, axis=0) * kbd_mask
        kq.append(_dot_nt(jnp.concatenate([kb_b[rows], q_b[rows]], axis=0), k_bd))
    x4, t4, aqk4 = [], [], []
    for ck, rows in enumerate(chunk_rows):
        n4 = jnp.where(strict4, kq[ck][0:c] * decay_all[rows], 0.0)
        aqk4.append(jnp.where(causal4, kq[ck][c:2 * c] * decay_all[rows], 0.0).astype(BF16))
        x4.append(n4)
        t4.append(eye4 - n4)
    for _ in range(n_dbl):
        x4 = [mm_packed(x, x) for x in x4]
        t4 = [t + mm_packed(t, x) for t, x in zip(t4, x4)]
    uw = []
    for ck, rows in enumerate(chunk_rows):
        t4b = t4[ck].astype(BF16)
        uw.append([_dot(t4b, pad_rows(rhs_b[h][rows], h)) for h in range(nh)])

    s_cur = [s_scr[h] for h in range(nh)]
    for ck, rows in enumerate(chunk_rows):
        ws = [_dot(jnp.concatenate([uw[ck][h][:, hd:2 * hd].astype(BF16), wq_b[h][rows]], axis=0),
                   s_cur[h].astype(BF16)) for h in range(nh)]
        v_new = [(uw[ck][h][:, 0:hd] - ws[h][0:c]).astype(BF16) for h in range(nh)]
        s_add = [_dot_tn(kd_b[h][rows], v_new[h]) for h in range(nh)]
        o_att = [_dot(aqk4[ck], pad_rows(v_new[h], h)) for h in range(nh)]
        for h in range(nh):
            sl = slice(h * hd, (h + 1) * hd)
            dl = jnp.broadcast_to(e_last[ck][:, nh + h:nh + h + 1], (hd, hd))
            s_cur[h] = s_cur[h] * dl + s_add[h]
            o_h = ws[h][c:2 * c] + o_att[h]
            o_n = o_h * lax.rsqrt(jnp.mean(o_h * o_h, axis=-1, keepdims=True) + 1e-6) * nw_ref[...]
            ya_ref[0, rows, sl] = o_n * _silu(z_ref[0, rows, sl])
    for h in range(nh):
        s_scr[h] = s_cur[h]

    @pl.when(ti == nt - 1)
    def _():
        sn_ref[0] = s_scr[...]


def _gdn(qkv, z, ba, buf, s0, conv_w, ad, norm_w, tt, chunk):
    b, t, _ = qkv.shape
    kern = functools.partial(_gdn_kernel, tt=tt, chunk=chunk)
    tile = lambda w: pl.BlockSpec((1, tt, w), lambda i, j: (i, j, 0))
    per_b3 = lambda s: pl.BlockSpec((1,) + s, lambda i, j: (i,) + (0,) * len(s))
    return pl.pallas_call(
        kern,
        grid=(b, t // tt),
        in_specs=[tile(QKV_COLS), tile(GDN_WIDTH), tile(LANES), per_b3((GDN_CONV - 1, QKV_COLS)),
                  per_b3((GDN_HEADS, GDN_HEAD_DIM, GDN_HEAD_DIM)), _const_spec(conv_w.shape),
                  _const_spec(ad.shape), _const_spec(norm_w.shape)],
        out_specs=[tile(GDN_WIDTH), per_b3((GDN_HEADS, GDN_HEAD_DIM, GDN_HEAD_DIM)),
                   per_b3((GDN_CONV - 1, QKV_COLS))],
        out_shape=[jax.ShapeDtypeStruct((b, t, GDN_WIDTH), F32),
                   jax.ShapeDtypeStruct((b, GDN_HEADS, GDN_HEAD_DIM, GDN_HEAD_DIM), F32),
                   jax.ShapeDtypeStruct((b, GDN_CONV - 1, QKV_COLS), F32)],
        scratch_shapes=[pltpu.VMEM((tt + 8, QKV_COLS), F32),
                        pltpu.VMEM((GDN_HEADS, GDN_HEAD_DIM, GDN_HEAD_DIM), F32),
                        pltpu.VMEM((tt, GDN_WIDTH), F32), pltpu.VMEM((tt, GDN_WIDTH), F32),
                        pltpu.VMEM((tt, GDN_WIDTH), F32), pltpu.VMEM((2, tt, LANES), F32)],
        compiler_params=_cparams(("parallel", "arbitrary")),
        name="gdn",
    )(qkv, z, ba, buf, s0, conv_w, ad, norm_w)


def _s5_prep_kernel(lam_ref, ldt_ref, bt_ref, cm_ref, ktoep_ref, bend_ref, cst_ref, lamc_ref):
    g_n, p_n, lc = S5_GROUPS, S5_STATE, S5_CHUNK
    lr, li = lam_ref[0], lam_ref[1]
    dt = jnp.exp(ldt_ref[...])
    mag = jnp.exp(lr * dt)
    ang = li * dt
    br, bi = mag * jnp.cos(ang), mag * jnp.sin(ang)
    den = lr * lr + li * li
    qr = ((br - 1.0) * lr + bi * li) / den
    qi = (bi * lr - (br - 1.0) * li) / den
    bbr = qr * bt_ref[0] - qi * bt_ref[1]
    bbi = qr * bt_ref[1] + qi * bt_ref[0]
    cr, cim = cm_ref[0], cm_ref[1]

    pw = [(jnp.ones_like(br), jnp.zeros_like(br))]
    for _ in range(lc):
        ar, ai = pw[-1]
        pw.append((ar * br - ai * bi, ar * bi + ai * br))
    lamc_ref[:, :, 0:p_n] = pw[lc][0]
    lamc_ref[:, :, p_n:2 * p_n] = pw[lc][1]

    for s in range(lc):
        ar, ai = pw[lc - 1 - s]
        bend_ref[:, s, :, 0:p_n] = ar * bbr - ai * bbi
        bend_ref[:, s, :, p_n:2 * p_n] = ar * bbi + ai * bbr
        ar, ai = pw[s + 1]
        cst_ref[:, s, :, 0:p_n] = ar * cr - ai * cim
        cst_ref[:, s, :, p_n:2 * p_n] = -(ar * cim + ai * cr)

    lane = lax.broadcasted_iota(jnp.int32, (S5_GROUP, lc * S5_GROUP), 1)
    for g in range(g_n):
        cpr = jnp.concatenate([pw[j][0][g] * cr[g] - pw[j][1][g] * cim[g] for j in range(lc)], axis=0)
        cpi = jnp.concatenate([pw[j][0][g] * cim[g] + pw[j][1][g] * cr[g] for j in range(lc)], axis=0)
        bh_r, bl_r = _split2(bbr[g])
        bh_i, bl_i = _split2(bbi[g])
        ch_r, cl_r = _split2(cpr)
        ch_i, cl_i = _split2(cpi)
        krow = (_dot_nt(bh_r, ch_r) + (_dot_nt(bh_r, cl_r) + _dot_nt(bl_r, ch_r))
                - (_dot_nt(bh_i, ch_i) + (_dot_nt(bh_i, cl_i) + _dot_nt(bl_i, ch_i))))
        for s in range(lc):
            sh = s * S5_GROUP
            blk = krow if s == 0 else jnp.where(lane >= sh, pltpu.roll(krow, sh, 1), 0.0)
            ktoep_ref[g, s * S5_GROUP:(s + 1) * S5_GROUP, :] = blk.astype(BF16)


def _s5_prep(lam, ldt, bt, cm):
    g_n, p_n, lc = S5_GROUPS, S5_STATE, S5_CHUNK
    w = lc * S5_GROUP
    return pl.pallas_call(
        _s5_prep_kernel,
        out_shape=[jax.ShapeDtypeStruct((g_n, w, w), BF16),
                   jax.ShapeDtypeStruct((g_n, lc, S5_GROUP, 2 * p_n), F32),
                   jax.ShapeDtypeStruct((g_n, lc, S5_GROUP, 2 * p_n), F32),
                   jax.ShapeDtypeStruct((g_n, 1, 2 * p_n), F32)],
        compiler_params=pltpu.CompilerParams(vmem_limit_bytes=VMEM_LIMIT),
        name="s5_prep",
    )(lam, ldt, bt, cm)


def _s5_kernel(u_ref, h0_ref, ktoep_ref, bend_ref, cst_ref, lamc_ref, d_ref, y_ref, hn_ref, *, n, nb):
    p_n = S5_STATE
    r_n = nb * n
    lane = lax.broadcasted_iota(jnp.int32, (1, 2 * p_n), 1)
    sign = jnp.where(lane < p_n, -1.0, 1.0)
    rowm = lax.broadcasted_iota(jnp.int32, (r_n, 2 * p_n), 0) % n
    if nb > 1:
        e_r = lax.broadcasted_iota(jnp.int32, (r_n, nb), 0)
        e_b = lax.broadcasted_iota(jnp.int32, (r_n, nb), 1)
        expand = (e_r // n == e_b).astype(BF16)
        s_b = lax.broadcasted_iota(jnp.int32, (nb, r_n), 0)
        s_r = lax.broadcasted_iota(jnp.int32, (nb, r_n), 1)
        pick_last = (s_r == s_b * n + (n - 1)).astype(BF16)

    def lam_tiles(g):
        lam = lamc_ref[pl.ds(g, 1), :]
        lam_sw = pltpu.roll(lam, p_n, 1)
        return (jnp.where(lane < p_n, lam, lam_sw),
                jnp.where(lane < p_n, lam_sw, lam) * sign)

    def cmul(x, a_rr, a_is):
        return x * a_rr + pltpu.roll(x, p_n, 1) * a_is

    def groups_body(gi, carry):
        gs = [gi * S5_UNROLL + j for j in range(S5_UNROLL)]
        us = [u_ref[0, g] for g in gs]
        lams = [lam_tiles(g) for g in gs]
        x_loc = [_dot_x3(u, bend_ref[g]) for u, g in zip(us, gs)]
        xs = []
        for g, xl in zip(gs, x_loc):
            h0 = h0_ref[0, g]
            h0_rows = _dot_exact_lhs(expand, h0) if nb > 1 else h0
            xs.append(jnp.where(rowm == 0, h0_rows, pltpu.roll(xl, 1, 0)))
        pw = list(lams)
        step = 1
        while step < n:
            xs = [x + cmul(jnp.where(rowm >= step, pltpu.roll(x, step, 0), 0.0), a_rr, a_is)
                  for x, (a_rr, a_is) in zip(xs, pw)]
            pw = [(a_rr * a_rr - a_is * a_is, 2.0 * a_rr * a_is) for a_rr, a_is in pw]
            step *= 2
        for g, u, x, xl, (b_rr, b_is) in zip(gs, us, xs, x_loc, lams):
            y = _dot(u.astype(BF16), ktoep_ref[g]) + _dot_nt(x.astype(BF16), cst_ref[g].astype(BF16))
            y = y + d_ref[pl.ds(g, 1), :] * u
            y_ref[0, g] = jax.nn.gelu(y)
            x_end = cmul(x, b_rr, b_is) + xl
            hn_ref[0, g] = _dot_exact_lhs(pick_last, x_end) if nb > 1 else x_end[n - 1:n, :]
        return carry

    lax.fori_loop(0, S5_GROUPS // S5_UNROLL, groups_body, 0)


def _s5(u, h0, ktoep, bend, cst, lamc, d_t, n):
    nblk, g_n, r_n, w = u.shape
    nb = r_n // n
    p2 = 2 * S5_STATE
    kern = functools.partial(_s5_kernel, n=n, nb=nb)
    return pl.pallas_call(
        kern,
        grid=(nblk,),
        in_specs=[pl.BlockSpec((1, g_n, r_n, w), lambda i: (i, 0, 0, 0)),
                  pl.BlockSpec((1, g_n, nb, p2), lambda i: (i, 0, 0, 0)),
                  _const_spec(ktoep.shape), _const_spec(bend.shape), _const_spec(cst.shape),
                  _const_spec(lamc.shape), _const_spec(d_t.shape)],
        out_specs=[pl.BlockSpec((1, g_n, r_n, w), lambda i: (i, 0, 0, 0)),
                   pl.BlockSpec((1, g_n, nb, p2), lambda i: (i, 0, 0, 0))],
        out_shape=[jax.ShapeDtypeStruct((nblk, g_n, r_n, w), F32),
                   jax.ShapeDtypeStruct((nblk, g_n, nb, p2), F32)],
        compiler_params=_cparams(("parallel",)),
        name="s5",
    )(u, h0, ktoep, bend, cst, lamc, d_t)


CC_HIST = CC_KERNEL - 1
CC_PAD = 32


def _cc_load_history(buf_ref, xbuf, *, ns, rs, first_tile):
    lo = CC_PAD - CC_HIST
    if ns == 1:
        @pl.when(first_tile)
        def _():
            xbuf[0, lo:CC_PAD, :] = buf_ref[0]

        @pl.when(jnp.logical_not(first_tile))
        def _():
            xbuf[0, lo:CC_PAD, :] = xbuf[0, rs + lo:rs + CC_PAD, :]
    else:
        for q in range(ns):
            xbuf[q, lo:CC_PAD, :] = buf_ref[q]


def _cc_module(cc, w_ref, p_ref, nb_ref, xbuf, sh_scr, *, ns, rs):
    x = cc[:, 0:CC_WIDTH] * jax.nn.sigmoid(cc[:, CC_WIDTH:])
    lo = CC_PAD - CC_HIST
    outs = []
    for q in range(ns):
        xbuf[q, CC_PAD:CC_PAD + rs, :] = x[q * rs:(q + 1) * rs]
        nb_ref[q] = xbuf[q, rs + lo:rs + CC_PAD, :]
        for a in range(1, 8):
            sh_scr[a - 1] = xbuf[q, a:a + rs + CC_PAD - 8, :]
        acc = None
        for j in range(CC_KERNEL):
            a, m = (lo + j) % 8, (lo + j) // 8
            win = sh_scr[a - 1, 8 * m:8 * m + rs, :] if a else xbuf[q, 8 * m:8 * m + rs, :]
            term = w_ref[j:j + 1, :] * win
            acc = term if acc is None else acc + term
        outs.append(acc)
    acc = (outs[0] if ns == 1 else jnp.concatenate(outs, axis=0)) + p_ref[0:1, :]
    mu = jnp.mean(acc, axis=-1, keepdims=True)
    xc = acc - mu
    var = jnp.mean(xc * xc, axis=-1, keepdims=True)
    return _silu(xc * lax.rsqrt(var + 1e-5) * p_ref[1:2, :] + p_ref[2:3, :])


FFN_SPLITS = ((0, 1024), (1024, 2048), (2048, FFN_HIDDEN))


def _out_ffn_kernel(h_ref, ya_ref, yf_ref, yc_ref, p_ref, gw_ref, gb_ref,
                    woa_ref, wob_ref, woc_ref, nf_ref, w1_ref, w3_ref, w2_ref, pew_ref, peg_ref, nfin_ref,
                    o_ref, yb_scr, ys_scr, *, tm, final):
    nck = tm // S5_CHUNK
    gpl = LANES // S5_GROUP
    for s in range(S5_CHUNK):
        for g in range(S5_GROUPS):
            ys_scr[s, g // gpl, :, (g % gpl) * S5_GROUP:(g % gpl + 1) * S5_GROUP] = (
                yf_ref[0, g, :, s * S5_GROUP:(s + 1) * S5_GROUP])
        for half in range(S5_WIDTH // LANES):
            yb_scr[half, pl.ds(s, nck, stride=S5_CHUNK), :] = ys_scr[s, half]
    yb = jnp.concatenate([yb_scr[half] for half in range(S5_WIDTH // LANES)], axis=1)
    yb = yb * jax.nn.sigmoid(_dot(yb.astype(BF16), gw_ref[...]) + gb_ref[...])
    h = h_ref[...] + (_dot(ya_ref[...].astype(BF16), woa_ref[...])
                      + _dot(yb.astype(BF16), wob_ref[...])
                      + _dot(yc_ref[...].astype(BF16), woc_ref[...]))
    ms = jnp.mean(h * h, axis=-1, keepdims=True)
    hf = ((h * lax.rsqrt(ms + 1e-6)) * nf_ref[...]).astype(BF16)
    for lo, hi in FFN_SPLITS:
        a = _dot(hf, w1_ref[:, lo:hi])
        b = _dot(hf, w3_ref[:, lo:hi])
        h = h + _dot((_silu(a) * b).astype(BF16), w2_ref[lo:hi, :])
    pe = _dot(p_ref[...].astype(BF16), pew_ref[...])
    h = h + pe * jax.nn.sigmoid(_dot(h.astype(BF16), peg_ref[...]))
    if final:
        ms = jnp.mean(h * h, axis=-1, keepdims=True)
        h = (h * lax.rsqrt(ms + 1e-6)) * nfin_ref[...]
    o_ref[...] = h


def _out_ffn(h, ya, yf, yc, p, consts, tm, tps, final):
    n = h.shape[0]
    row = lambda w: pl.BlockSpec((tm, w), lambda i: (i, 0))
    return pl.pallas_call(
        functools.partial(_out_ffn_kernel, tm=tm, final=final),
        grid=(n // tm,),
        in_specs=[row(D_MODEL), row(GDN_WIDTH), _s5_block_spec(tm, tps), row(CC_WIDTH), row(PLE_DIM)]
        + [_const_spec(a.shape) for a in consts],
        out_specs=row(D_MODEL),
        out_shape=jax.ShapeDtypeStruct((n, D_MODEL), F32),
        scratch_shapes=[pltpu.VMEM((S5_WIDTH // LANES, tm, LANES), F32),
                        pltpu.VMEM((S5_CHUNK, S5_WIDTH // LANES, tm // S5_CHUNK, LANES), F32)],
        compiler_params=_cparams(("parallel",)),
        name="out_ffn",
    )(h, ya, yf, yc, p, *consts)


def _tile_rows(n, want):
    t = min(n, want)
    while n % t:
        t //= 2
    return t


def _prep_layer(i, W):
    w_in = W['w_in'][i]
    L = {}
    L['g_mix'] = W['norm_mix'][i][None, :]
    L['wq'] = w_in[:, :OFF_Z].astype(BF16)
    L['wz'] = w_in[:, OFF_Z:OFF_BA].astype(BF16)
    L['wba'] = jnp.pad(w_in[:, OFF_BA:OFF_S5], ((0, 0), (0, LANES - 2 * GDN_HEADS))).astype(BF16)
    L['wu'] = w_in[:, OFF_S5:OFF_CC].astype(BF16)
    L['wc'] = w_in[:, OFF_CC:].astype(BF16)
    L['conv_w'] = W['gdn_conv_w'][i]
    padl = (GDN_HEADS, LANES - 2 * GDN_HEADS)
    L['ad'] = jnp.stack([jnp.pad(W['gdn_a_log'][i], padl), jnp.pad(W['gdn_dt_bias'][i], padl)])
    L['gdn_norm'] = W['gdn_norm'][i][None, :]
    lam = jnp.stack([W['s5_lam_re'][i], W['s5_lam_im'][i]])[:, :, None, :]
    bt =jnp.stack([jnp.swapaxes(W['s5_b_re'][i], 1, 2), jnp.swapaxes(W['s5_b_im'][i], 1, 2)])
    cm = jnp.stack([W['s5_c_re'][i], W['s5_c_im'][i]])
    ktoep, bend, cst, lamc = _s5_prep(lam, W['s5_log_dt'][i][:, None, None], bt, cm)
    w = S5_CHUNK * S5_GROUP
    L['ktoep'] = ktoep
    L['bend'] = bend.reshape(S5_GROUPS, w, 2 * S5_STATE)
    L['cst'] = cst.reshape(S5_GROUPS, w, 2 * S5_STATE)
    L['lamc'] = lamc.reshape(S5_GROUPS, 2 * S5_STATE)
    L['d_t'] = jnp.tile(W['s5_d'][i].reshape(S5_GROUPS, S5_GROUP), (1, S5_CHUNK))
    L['glu_w'] = W['s5_glu_w'][i].astype(BF16)
    L['glu_b'] = W['s5_glu_b'][i][None, :]
    L['dw_w'] = W['cc_dw_w'][i]
    L['cc_p'] = jnp.stack([W['cc_dw_b'][i], W['cc_ln_g'][i], W['cc_ln_b'][i]])
    wo = W['w_out'][i].astype(BF16)
    L['woa'], L['wob'], L['woc'] = wo[:GDN_WIDTH], wo[GDN_WIDTH:GDN_WIDTH + S5_WIDTH], wo[GDN_WIDTH + S5_WIDTH:]
    L['g_ffn'] = W['norm_ffn'][i][None, :]
    L['w1'] = W['ffn_w1'][i].astype(BF16)
    L['w3'] = W['ffn_w3'][i].astype(BF16)
    L['w2'] = W['ffn_w2'][i].astype(BF16)
    L['pew'] = W['pe_w'][i].astype(BF16)
    L['peg'] = W['pe_gate_w'][i].astype(BF16)
    L['g_fin'] = W['norm_final'][None, :]
    return L


def _layer(h, p, st_gdn, st_gconv, st_s5, st_conv, L, final):
    b, t, _ = h.shape
    n = b * t
    tm = _tile_rows(n, ROW_TILE)
    ns, tps = _seq_tiling(b, t, tm)
    qkv, z, ba, u_f, yc, b_conv = _in_proj(h.reshape(n, D_MODEL), L['g_mix'], st_conv, L['wq'], L['wz'], L['wba'],
                                           L['wu'], L['wc'], L['dw_w'], L['cc_p'], tm, ns, tps)

    chunk = 64 if t % 64 == 0 else t
    tt = _tile_rows(t, 512)
    ya, s_gdn, b_gconv = _gdn(qkv.reshape(b, t, QKV_COLS), z.reshape(b, t, GDN_WIDTH), ba.reshape(b, t, LANES),
                              st_gconv, st_gdn, L['conv_w'], L['ad'], L['gdn_norm'], tt, chunk)

    h0 = jnp.concatenate([st_s5[..., 0], st_s5[..., 1]], axis=-1)
    h0 = jnp.swapaxes(h0.reshape(b // ns, ns, S5_GROUPS, 2 * S5_STATE), 1, 2)
    y_f, hn = _s5(u_f, h0, L['ktoep'], L['bend'], L['cst'], L['lamc'], L['d_t'], t // S5_CHUNK)
    hn = jnp.swapaxes(hn, 1, 2).reshape(b, S5_GROUPS, 2 * S5_STATE)
    s_s5 = jnp.stack([hn[..., :S5_STATE], hn[..., S5_STATE:]], axis=-1)

    consts = [L['glu_w'], L['glu_b'], L['woa'], L['wob'], L['woc'], L['g_ffn'],
              L['w1'], L['w3'], L['w2'], L['pew'], L['peg'], L['g_fin']]
    h_new = _out_ffn(h.reshape(n, D_MODEL), ya.reshape(n, GDN_WIDTH), y_f, yc, p.reshape(n, PLE_DIM),
                     consts, tm, tps, final)
    return h_new.reshape(b, t, D_MODEL), s_gdn, b_gconv, s_s5, b_conv


def _trunk(x, p, st_gdn, st_gconv, st_s5, st_conv, layers):
    h = x
    outs = ([], [], [], [])
    depth = len(layers)
    for i, L in enumerate(layers):
        h, *st = _layer(h, p[i], st_gdn[i], st_gconv[i], st_s5[i], st_conv[i], L, i == depth - 1)
        for acc, s in zip(outs, st):
            acc.append(s)
    return (h,) + tuple(jnp.stack(o) for o in outs)


def kernel(x_prompt, x_sample, p_prompt, p_sample, state_gdn, state_gdn_conv, state_s5, state_conv, norm_mix, w_in, gdn_conv_w, gdn_a_log, gdn_dt_bias, gdn_norm, s5_lam_re, s5_lam_im, s5_log_dt, s5_b_re, s5_b_im, s5_c_re, s5_c_im, s5_d, s5_glu_w, s5_glu_b, cc_dw_w, cc_dw_b, cc_ln_g, cc_ln_b, w_out, norm_ffn, ffn_w1, ffn_w3, ffn_w2, pe_w, pe_gate_w, norm_final):
    W = dict(norm_mix=norm_mix, w_in=w_in, gdn_conv_w=gdn_conv_w, gdn_a_log=gdn_a_log, gdn_dt_bias=gdn_dt_bias,
             gdn_norm=gdn_norm, s5_lam_re=s5_lam_re, s5_lam_im=s5_lam_im, s5_log_dt=s5_log_dt, s5_b_re=s5_b_re,
             s5_b_im=s5_b_im, s5_c_re=s5_c_re, s5_c_im=s5_c_im, s5_d=s5_d, s5_glu_w=s5_glu_w, s5_glu_b=s5_glu_b,
             cc_dw_w=cc_dw_w, cc_dw_b=cc_dw_b, cc_ln_g=cc_ln_g, cc_ln_b=cc_ln_b, w_out=w_out, norm_ffn=norm_ffn,
             ffn_w1=ffn_w1, ffn_w3=ffn_w3, ffn_w2=ffn_w2, pe_w=pe_w, pe_gate_w=pe_gate_w, norm_final=norm_final)
    depth = w_in.shape[0]
    layers = [_prep_layer(i, W) for i in range(depth)]
    bp = x_prompt.shape[0]
    z_gdn = jnp.zeros((depth, bp, GDN_HEADS, GDN_HEAD_DIM, GDN_HEAD_DIM), F32)
    z_gconv = jnp.zeros((depth, bp, GDN_CONV - 1, QKV_COLS), F32)
    z_s5 = jnp.zeros((depth, bp, S5_GROUPS, S5_STATE, 2), F32)
    z_conv = jnp.zeros((depth, bp, CC_KERNEL - 1, CC_WIDTH), F32)
    y_p, gdn_p, gconv_p, s5_p, conv_p = _trunk(x_prompt, p_prompt, z_gdn, z_gconv, z_s5, z_conv, layers)
    y_s, gdn_s, gconv_s, s5_s, conv_s = _trunk(x_sample, p_sample, state_gdn, state_gdn_conv, state_s5,
                                               state_conv, layers)
    return (y_p, y_s, gdn_p, gconv_p, s5_p, conv_p, gdn_s, gconv_s, s5_s, conv_s)
```

```python
import functools
import math

import jax
import jax.numpy as jnp
from jax import lax
from jax.experimental import pallas as pl
from jax.experimental.pallas import tpu as pltpu

F32 = jnp.float32
BF16 = jnp.bfloat16

D_MODEL = 1024
GDN_HEADS = 4
GDN_HEAD_DIM = 128
GDN_WIDTH = GDN_HEADS * GDN_HEAD_DIM
GDN_CONV = 4
QKV_COLS = 3 * GDN_WIDTH
S5_WIDTH = 256
S5_GROUP = 16
S5_GROUPS = 16
S5_STATE = 64
S5_CHUNK = 16
S5_UNROLL = 4
CC_WIDTH = 256
CC_KERNEL = 31
FFN_HIDDEN = 2816
ROW_TILE = 512
PLE_DIM = 256
OFF_Z = QKV_COLS
OFF_BA = OFF_Z + GDN_WIDTH
OFF_S5 = OFF_BA + 2 * GDN_HEADS
OFF_CC = OFF_S5 + S5_WIDTH
IN_COLS = OFF_CC + 2 * CC_WIDTH
LANES = 128
VMEM_LIMIT = 56 * 1024 * 1024


def _cparams(sem):
    return pltpu.CompilerParams(dimension_semantics=sem, vmem_limit_bytes=VMEM_LIMIT)


def _const_spec(shape):
    nd = len(shape)
    return pl.BlockSpec(shape, lambda *_: (0,) * nd, pipeline_mode=pl.Buffered(1))


def _split2(x):
    hi = x.astype(BF16)
    lo = (x - hi.astype(F32)).astype(BF16)
    return hi, lo


def _split3(x):
    hi = x.astype(BF16)
    r = x - hi.astype(F32)
    mid = r.astype(BF16)
    lo = (r - mid.astype(F32)).astype(BF16)
    return hi, mid, lo


def _dot(a, b):
    return jnp.dot(a, b, preferred_element_type=F32)


def _dot_nt(a, b):
    return lax.dot_general(a, b, (((1,), (1,)), ((), ())), preferred_element_type=F32)


def _dot_tn(a, b):
    return lax.dot_general(a, b, (((0,), (0,)), ((), ())), preferred_element_type=F32)


def _dot_x3(a, b):
    ah, al = _split2(a)
    bh, bl = _split2(b)
    return _dot(ah, bh) + (_dot(ah, bl) + _dot(al, bh))


def _dot_exact_lhs(l_bf16, x):
    h, m, lo = _split3(x)
    return _dot(l_bf16, h) + (_dot(l_bf16, m) + _dot(l_bf16, lo))


def _silu(x):
    return x * jax.nn.sigmoid(x)


def _seq_tiling(b, t, tm):
    if t >= tm:
        assert t % tm == 0
        return 1, t // tm
    assert tm % t == 0 and (b * t) % tm == 0
    return tm // t, 1


def _s5_block_spec(tm, tps):
    return pl.BlockSpec((1, S5_GROUPS, tm // S5_CHUNK, S5_CHUNK * S5_GROUP), lambda i: (i // tps, 0, i % tps, 0))


def _inproj_kernel(h_ref, g_ref, cbuf_ref, wq_ref, wz_ref, wba_ref, wu_ref, wc_ref, dw_ref, ccp_ref,
                   qkv_ref, z_ref, ba_ref, uf_ref, yc_ref, nb_ref, u_scr, xbuf, sh_scr, *, tm, ns, tps):
    _cc_load_history(cbuf_ref, xbuf, ns=ns, rs=tm // ns, first_tile=pl.program_id(0) % tps == 0)
    x = h_ref[...]
    ms = jnp.mean(x * x, axis=-1, keepdims=True)
    hn = (x * lax.rsqrt(ms + 1e-6)) * g_ref[...]
    hb = hn.astype(BF16)
    u = _dot(hb, wu_ref[...])
    nck = tm // S5_CHUNK
    gpl = LANES // S5_GROUP
    for half in range(S5_WIDTH // LANES):
        u_scr[half] = u[:, half * LANES:(half + 1) * LANES]
    cc = _dot(hb, wc_ref[...])
    for s in range(S5_CHUNK):
        for half in range(S5_WIDTH // LANES):
            rows = u_scr[half, pl.ds(s, nck, stride=S5_CHUNK), :]
            for g in range(gpl):
                uf_ref[0, half * gpl + g, :, s * S5_GROUP:(s + 1) * S5_GROUP] = (
                    rows[:, g * S5_GROUP:(g + 1) * S5_GROUP])
    qkv_ref[...] = _dot(hb, wq_ref[...])
    z_ref[...] = _dot(hb, wz_ref[...])
    ba_ref[...] = _dot(hb, wba_ref[...])
    yc_ref[...] = _cc_module(cc, dw_ref, ccp_ref, nb_ref, xbuf, sh_scr, ns=ns, rs=tm // ns).astype(BF16)


def _in_proj(h, g, cbuf, wq, wz, wba, wu, wc, dw_w, cc_p, tm, ns, tps):
    n = h.shape[0]
    row = lambda w: pl.BlockSpec((tm, w), lambda i: (i, 0))
    seq_blk = pl.BlockSpec((ns, CC_HIST, CC_WIDTH), lambda i: (i // tps, 0, 0))
    nblk = n // (tm * tps)
    rs = tm // ns
    consts = [wq, wz, wba, wu, wc, dw_w, cc_p]
    return pl.pallas_call(
        functools.partial(_inproj_kernel, tm=tm, ns=ns, tps=tps),
        grid=(n // tm,),
        in_specs=[row(D_MODEL), _const_spec((1, D_MODEL)), seq_blk] + [_const_spec(a.shape) for a in consts],
        out_specs=[row(QKV_COLS), row(GDN_WIDTH), row(LANES), _s5_block_spec(tm, tps), row(CC_WIDTH), seq_blk],
        out_shape=[jax.ShapeDtypeStruct((n, QKV_COLS), F32), jax.ShapeDtypeStruct((n, GDN_WIDTH), F32),
                   jax.ShapeDtypeStruct((n, LANES), F32),
                   jax.ShapeDtypeStruct((nblk, S5_GROUPS, tps * tm // S5_CHUNK, S5_CHUNK * S5_GROUP), F32),
                   jax.ShapeDtypeStruct((n, CC_WIDTH), BF16), jax.ShapeDtypeStruct(cbuf.shape, F32)],
        scratch_shapes=[pltpu.VMEM((S5_WIDTH // LANES, tm, LANES), F32),
                        pltpu.VMEM((ns, rs + CC_PAD, CC_WIDTH), F32),
                        pltpu.VMEM((7, rs + CC_PAD - 8, CC_WIDTH), F32)],
        compiler_params=_cparams(("arbitrary",)),
        name="in_proj",
    )(h, g, cbuf, *consts)


def _gdn_kernel(qkv_ref, z_ref, ba_ref, buf_ref, s0_ref, cw_ref, ad_ref, nw_ref,
                ya_ref, sn_ref, nb_ref,
                xbuf, s_scr, q_scr, k_scr, v_scr, gb_scr, *, tt, chunk):
    ti = pl.program_id(1)
    nt = pl.num_programs(1)
    hd = GDN_HEAD_DIM
    nh = GDN_HEADS
    c = chunk
    pad = 8
    hist = GDN_CONV - 1

    @pl.when(ti == 0)
    def _():
        xbuf[pad - hist:pad, :] = buf_ref[0]
        s_scr[...] = s0_ref[0]

    @pl.when(ti > 0)
    def _():
        xbuf[pad - hist:pad, :] = xbuf[pad + tt - hist:pad + tt, :]

    xbuf[pad:pad + tt, :] = qkv_ref[0]
    nb_ref[0] = xbuf[pad + tt - hist:pad + tt, :]

    for cb in range(3 * nh):
        cols = slice(cb * hd, (cb + 1) * hd)
        acc = cw_ref[0:1, cols] * xbuf[pad - hist:pad - hist + tt, cols]
        for j in range(1, GDN_CONV):
            acc = acc + cw_ref[j:j + 1, cols] * xbuf[pad - hist + j:pad - hist + j + tt, cols]
        y = _silu(acc)
        h = cb % nh
        if cb < nh:
            q_scr[:, h * hd:(h + 1) * hd] = y * (lax.rsqrt(jnp.sum(y * y, axis=-1, keepdims=True) + 1e-6)
                                                 * (hd ** -0.5))
        elif cb < 2 * nh:
            k_scr[:, h * hd:(h + 1) * hd] = y * lax.rsqrt(jnp.sum(y * y, axis=-1, keepdims=True) + 1e-6)
        else:
            v_scr[:, h * hd:(h + 1) * hd] = y

    ba = ba_ref[0]
    beta = jax.nn.sigmoid(ba)
    sp_in = ba + ad_ref[1:2, :]
    softplus = jnp.maximum(sp_in, 0.0) + jnp.log(1.0 + jnp.exp(-jnp.abs(sp_in)))
    g = -jnp.exp(ad_ref[0:1, :]) * softplus
    gb_scr[0] = beta
    gb_scr[1] = g

    nck = tt // c
    t_r = lax.broadcasted_iota(jnp.int32, (tt, tt), 0)
    t_c = lax.broadcasted_iota(jnp.int32, (tt, tt), 1)
    tril_bd = ((t_r // c == t_c // c) & (t_r >= t_c)).astype(BF16)
    pi = lax.broadcasted_iota(jnp.int32, (tt, nh * c), 0) % c
    pc = lax.broadcasted_iota(jnp.int32, (tt, nh * c), 1)
    ps = pc % c
    p_head = pc // c
    ci_r = lax.broadcasted_iota(jnp.int32, (c, nh * c), 0)
    ci_s = lax.broadcasted_iota(jnp.int32, (c, nh * c), 1) % c
    strict4 = ci_r > ci_s
    causal4 = ci_r >= ci_s
    eye4 = (ci_r == ci_s).astype(F32)
    bd_r = lax.broadcasted_iota(jnp.int32, (nh * c, nh * c), 0) // c
    bd_c = lax.broadcasted_iota(jnp.int32, (nh * c, nh * c), 1) // c
    bd_mask = (bd_r == bd_c).astype(BF16)
    kbd_r = lax.broadcasted_iota(jnp.int32, (nh * c, GDN_WIDTH), 0) // c
    kbd_c = lax.broadcasted_iota(jnp.int32, (nh * c, GDN_WIDTH), 1) // hd
    kbd_mask = (kbd_r == kbd_c).astype(BF16)
    n_dbl = int(math.log2(c)) - 1

    def blockdiag(xb):
        return jnp.concatenate([xb] * nh, axis=0) * bd_mask

    def mm_packed(a4, b4):
        return _dot(a4.astype(BF16), blockdiag(b4.astype(BF16)))

    def pad_rows(x, h):
        z = jnp.zeros_like(x)
        return jnp.concatenate([x if j == h else z for j in range(nh)], axis=0)

    g_all = gb_scr[1]
    gc_all = _dot_exact_lhs(tril_bd, g_all)
    g4 = jnp.zeros((tt, nh * c), F32)
    for h in range(nh):
        g4 = jnp.where(p_head == h, jnp.broadcast_to(g_all[:, nh + h:nh + h + 1], (tt, nh * c)), g4)
    decay_all = jnp.exp(_dot_exact_lhs(tril_bd, jnp.where(pi > ps, g4, 0.0)))

    chunk_rows = [slice(ck * c, (ck + 1) * c) for ck in range(nck)]
    g_last_b = jnp.concatenate(
        [jnp.broadcast_to(gc_all[(ck + 1) * c - 1:(ck + 1) * c], (c, LANES)) for ck in range(nck)], axis=0)
    e_gc = jnp.exp(gc_all)
    e_rem = jnp.exp(g_last_b - gc_all)
    e_last = [jnp.exp(gc_all[(ck + 1) * c - 1:(ck + 1) * c]) for ck in range(nck)]
    beta_all = gb_scr[0]
    q_all = q_scr[...]
    k_all = k_scr[...]
    v_all = v_scr[...]
    kb_parts, wq_b, kd_b, rhs_b = [], [], [], []
    for h in range(nh):
        sl = slice(h * hd, (h + 1) * hd)
        b_h = jnp.broadcast_to(beta_all[:, h:h + 1], (tt, hd))
        eg_h = jnp.broadcast_to(e_gc[:, nh + h:nh + h + 1], (tt, hd))
        er_h = jnp.broadcast_to(e_rem[:, nh + h:nh + h + 1], (tt, hd))
        kb_h = k_all[:, sl] * b_h
        kb_parts.append(kb_h)
        wq_b.append((q_all[:, sl] * eg_h).astype(BF16))
        kd_b.append((k_all[:, sl] * er_h).astype(BF16))
        rhs_b.append(jnp.concatenate([v_all[:, sl] * b_h, kb_h * eg_h], axis=1).astype(BF16))
    kb_b = jnp.concatenate(kb_parts, axis=1).astype(BF16)
    q_b = q_all.astype(BF16)
    k_b = k_all.astype(BF16)

    kq = []
    for rows in chunk_rows:
        k_bd = jnp.concatenate([k_b[rows]] * nh, axis=0) * kbd_mask
        kq.append(_dot_nt(jnp.concatenate([kb_b[rows], q_b[rows]], axis=0), k_bd))
    x4, t4, aqk4 = [], [], []
    for ck, rows in enumerate(chunk_rows):
        n4 = jnp.where(strict4, kq[ck][0:c] * decay_all[rows], 0.0)
        aqk4.append(jnp.where(causal4, kq[ck][c:2 * c] * decay_all[rows], 0.0).astype(BF16))
        x4.append(n4)
        t4.append(eye4 - n4)
    for _ in range(n_dbl):
        x4 = [mm_packed(x, x) for x in x4]
        t4 = [t + mm_packed(t, x) for t, x in zip(t4, x4)]
    uw = []
    for ck, rows in enumerate(chunk_rows):
        t4b = t4[ck].astype(BF16)
        uw.append([_dot(t4b, pad_rows(rhs_b[h][rows], h)) for h in range(nh)])

    s_cur = [s_scr[h] for h in range(nh)]
    for ck, rows in enumerate(chunk_rows):
        ws = [_dot(jnp.concatenate([uw[ck][h][:, hd:2 * hd].astype(BF16), wq_b[h][rows]], axis=0),
                   s_cur[h].astype(BF16)) for h in range(nh)]
        v_new = [(uw[ck][h][:, 0:hd] - ws[h][0:c]).astype(BF16) for h in range(nh)]
        s_add = [_dot_tn(kd_b[h][rows], v_new[h]) for h in range(nh)]
        o_att = [_dot(aqk4[ck], pad_rows(v_new[h], h)) for h in range(nh)]
        for h in range(nh):
            sl = slice(h * hd, (h + 1) * hd)
            dl = jnp.broadcast_to(e_last[ck][:, nh + h:nh + h + 1], (hd, hd))
            s_cur[h] = s_cur[h] * dl + s_add[h]
            o_h = ws[h][c:2 * c] + o_att[h]
            o_n = o_h * lax.rsqrt(jnp.mean(o_h * o_h, axis=-1, keepdims=True) + 1e-6) * nw_ref[...]
            ya_ref[0, rows, sl] = (o_n * _silu(z_ref[0, rows, sl])).astype(BF16)
    for h in range(nh):
        s_scr[h] = s_cur[h]

    @pl.when(ti == nt - 1)
    def _():
        sn_ref[0] = s_scr[...]


def _gdn(qkv, z, ba, buf, s0, conv_w, ad, norm_w, tt, chunk):
    b, t, _ = qkv.shape
    kern = functools.partial(_gdn_kernel, tt=tt, chunk=chunk)
    tile = lambda w: pl.BlockSpec((1, tt, w), lambda i, j: (i, j, 0))
    per_b3 = lambda s: pl.BlockSpec((1,) + s, lambda i, j: (i,) + (0,) * len(s))
    return pl.pallas_call(
        kern,
        grid=(b, t // tt),
        in_specs=[tile(QKV_COLS), tile(GDN_WIDTH), tile(LANES), per_b3((GDN_CONV - 1, QKV_COLS)),
                  per_b3((GDN_HEADS, GDN_HEAD_DIM, GDN_HEAD_DIM)), _const_spec(conv_w.shape),
                  _const_spec(ad.shape), _const_spec(norm_w.shape)],
        out_specs=[tile(GDN_WIDTH), per_b3((GDN_HEADS, GDN_HEAD_DIM, GDN_HEAD_DIM)),
                   per_b3((GDN_CONV - 1, QKV_COLS))],
        out_shape=[jax.ShapeDtypeStruct((b, t, GDN_WIDTH), BF16),
                   jax.ShapeDtypeStruct((b, GDN_HEADS, GDN_HEAD_DIM, GDN_HEAD_DIM), F32),
                   jax.ShapeDtypeStruct((b, GDN_CONV - 1, QKV_COLS), F32)],
        scratch_shapes=[pltpu.VMEM((tt + 8, QKV_COLS), F32),
                        pltpu.VMEM((GDN_HEADS, GDN_HEAD_DIM, GDN_HEAD_DIM), F32),
                        pltpu.VMEM((tt, GDN_WIDTH), F32), pltpu.VMEM((tt, GDN_WIDTH), F32),
                        pltpu.VMEM((tt, GDN_WIDTH), F32), pltpu.VMEM((2, tt, LANES), F32)],
        compiler_params=_cparams(("parallel", "arbitrary")),
        name="gdn",
    )(qkv, z, ba, buf, s0, conv_w, ad, norm_w)


def _s5_prep_kernel(lam_ref, ldt_ref, bt_ref, cm_ref, ktoep_ref, bend_ref, cst_ref, lamc_ref):
    g_n, p_n, lc = S5_GROUPS, S5_STATE, S5_CHUNK
    lr, li = lam_ref[0], lam_ref[1]
    dt = jnp.exp(ldt_ref[...])
    mag = jnp.exp(lr * dt)
    ang = li * dt
    br, bi = mag * jnp.cos(ang), mag * jnp.sin(ang)
    den = lr * lr + li * li
    qr = ((br - 1.0) * lr + bi * li) / den
    qi = (bi * lr - (br - 1.0) * li) / den
    bbr = qr * bt_ref[0] - qi * bt_ref[1]
    bbi = qr * bt_ref[1] + qi * bt_ref[0]
    cr, cim = cm_ref[0], cm_ref[1]

    pw = [(jnp.ones_like(br), jnp.zeros_like(br))]
    for _ in range(lc):
        ar, ai = pw[-1]
        pw.append((ar * br - ai * bi, ar * bi + ai * br))
    lamc_ref[:, :, 0:p_n] = pw[lc][0]
    lamc_ref[:, :, p_n:2 * p_n] = pw[lc][1]

    for s in range(lc):
        ar, ai = pw[lc - 1 - s]
        bend_ref[:, s, :, 0:p_n] = ar * bbr - ai * bbi
        bend_ref[:, s, :, p_n:2 * p_n] = ar * bbi + ai * bbr
        ar, ai = pw[s + 1]
        cst_ref[:, s, :, 0:p_n] = ar * cr - ai * cim
        cst_ref[:, s, :, p_n:2 * p_n] = -(ar * cim + ai * cr)

    lane = lax.broadcasted_iota(jnp.int32, (S5_GROUP, lc * S5_GROUP), 1)
    for g in range(g_n):
        cpr = jnp.concatenate([pw[j][0][g] * cr[g] - pw[j][1][g] * cim[g] for j in range(lc)], axis=0)
        cpi = jnp.concatenate([pw[j][0][g] * cim[g] + pw[j][1][g] * cr[g] for j in range(lc)], axis=0)
        bh_r, bl_r = _split2(bbr[g])
        bh_i, bl_i = _split2(bbi[g])
        ch_r, cl_r = _split2(cpr)
        ch_i, cl_i = _split2(cpi)
        krow = (_dot_nt(bh_r, ch_r) + (_dot_nt(bh_r, cl_r) + _dot_nt(bl_r, ch_r))
                - (_dot_nt(bh_i, ch_i) + (_dot_nt(bh_i, cl_i) + _dot_nt(bl_i, ch_i))))
        for s in range(lc):
            sh = s * S5_GROUP
            blk = krow if s == 0 else jnp.where(lane >= sh, pltpu.roll(krow, sh, 1), 0.0)
            ktoep_ref[g, s * S5_GROUP:(s + 1) * S5_GROUP, :] = blk.astype(BF16)


def _s5_prep(lam, ldt, bt, cm):
    g_n, p_n, lc = S5_GROUPS, S5_STATE, S5_CHUNK
    w = lc * S5_GROUP
    return pl.pallas_call(
        _s5_prep_kernel,
        out_shape=[jax.ShapeDtypeStruct((g_n, w, w), BF16),
                   jax.ShapeDtypeStruct((g_n, lc, S5_GROUP, 2 * p_n), F32),
                   jax.ShapeDtypeStruct((g_n, lc, S5_GROUP, 2 * p_n), F32),
                   jax.ShapeDtypeStruct((g_n, 1, 2 * p_n), F32)],
        compiler_params=pltpu.CompilerParams(vmem_limit_bytes=VMEM_LIMIT),
        name="s5_prep",
    )(lam, ldt, bt, cm)


def _s5_kernel(u_ref, h0_ref, ktoep_ref, bend_ref, cst_ref, lamc_ref, d_ref, y_ref, hn_ref, *, n, nb):
    p_n = S5_STATE
    r_n = nb * n
    lane = lax.broadcasted_iota(jnp.int32, (1, 2 * p_n), 1)
    sign = jnp.where(lane < p_n, -1.0, 1.0)
    rowm = lax.broadcasted_iota(jnp.int32, (r_n, 2 * p_n), 0) % n
    if nb > 1:
        e_r = lax.broadcasted_iota(jnp.int32, (r_n, nb), 0)
        e_b = lax.broadcasted_iota(jnp.int32, (r_n, nb), 1)
        expand = (e_r // n == e_b).astype(BF16)
        s_b = lax.broadcasted_iota(jnp.int32, (nb, r_n), 0)
        s_r = lax.broadcasted_iota(jnp.int32, (nb, r_n), 1)
        pick_last = (s_r == s_b * n + (n - 1)).astype(BF16)

    def lam_tiles(g):
        lam = lamc_ref[pl.ds(g, 1), :]
        lam_sw = pltpu.roll(lam, p_n, 1)
        return (jnp.where(lane < p_n, lam, lam_sw),
                jnp.where(lane < p_n, lam_sw, lam) * sign)

    def cmul(x, a_rr, a_is):
        return x * a_rr + pltpu.roll(x, p_n, 1) * a_is

    def groups_body(gi, carry):
        gs = [gi * S5_UNROLL + j for j in range(S5_UNROLL)]
        us = [u_ref[0, g] for g in gs]
        lams = [lam_tiles(g) for g in gs]
        x_loc = [_dot_x3(u, bend_ref[g]) for u, g in zip(us, gs)]
        xs = []
        for g, xl in zip(gs, x_loc):
            h0 = h0_ref[0, g]
            h0_rows = _dot_exact_lhs(expand, h0) if nb > 1 else h0
            xs.append(jnp.where(rowm == 0, h0_rows, pltpu.roll(xl, 1, 0)))
        pw = list(lams)
        step = 1
        while step < n:
            xs = [x + cmul(jnp.where(rowm >= step, pltpu.roll(x, step, 0), 0.0), a_rr, a_is)
                  for x, (a_rr, a_is) in zip(xs, pw)]
            pw = [(a_rr * a_rr - a_is * a_is, 2.0 * a_rr * a_is) for a_rr, a_is in pw]
            step *= 2
        for g, u, x, xl, (b_rr, b_is) in zip(gs, us, xs, x_loc, lams):
            y = _dot(u.astype(BF16), ktoep_ref[g]) + _dot_nt(x.astype(BF16), cst_ref[g].astype(BF16))
            y = y + d_ref[pl.ds(g, 1), :] * u
            y_ref[0, g] = jax.nn.gelu(y)
            x_end = cmul(x, b_rr, b_is) + xl
            hn_ref[0, g] = _dot_exact_lhs(pick_last, x_end) if nb > 1 else x_end[n - 1:n, :]
        return carry

    lax.fori_loop(0, S5_GROUPS // S5_UNROLL, groups_body, 0)


def _s5(u, h0, ktoep, bend, cst, lamc, d_t, n):
    nblk, g_n, r_n, w = u.shape
    nb = r_n // n
    p2 = 2 * S5_STATE
    kern = functools.partial(_s5_kernel, n=n, nb=nb)
    return pl.pallas_call(
        kern,
        grid=(nblk,),
        in_specs=[pl.BlockSpec((1, g_n, r_n, w), lambda i: (i, 0, 0, 0)),
                  pl.BlockSpec((1, g_n, nb, p2), lambda i: (i, 0, 0, 0)),
                  _const_spec(ktoep.shape), _const_spec(bend.shape), _const_spec(cst.shape),
                  _const_spec(lamc.shape), _const_spec(d_t.shape)],
        out_specs=[pl.BlockSpec((1, g_n, r_n, w), lambda i: (i, 0, 0, 0)),
                   pl.BlockSpec((1, g_n, nb, p2), lambda i: (i, 0, 0, 0))],
        out_shape=[jax.ShapeDtypeStruct((nblk, g_n, r_n, w), F32),
                   jax.ShapeDtypeStruct((nblk, g_n, nb, p2), F32)],
        compiler_params=_cparams(("parallel",)),
        name="s5",
    )(u, h0, ktoep, bend, cst, lamc, d_t)


CC_HIST = CC_KERNEL - 1
CC_PAD = 32


def _cc_load_history(buf_ref, xbuf, *, ns, rs, first_tile):
    lo = CC_PAD - CC_HIST
    if ns == 1:
        @pl.when(first_tile)
        def _():
            xbuf[0, lo:CC_PAD, :] = buf_ref[0]

        @pl.when(jnp.logical_not(first_tile))
        def _():
            xbuf[0, lo:CC_PAD, :] = xbuf[0, rs + lo:rs + CC_PAD, :]
    else:
        for q in range(ns):
            xbuf[q, lo:CC_PAD, :] = buf_ref[q]


def _cc_module(cc, w_ref, p_ref, nb_ref, xbuf, sh_scr, *, ns, rs):
    x = cc[:, 0:CC_WIDTH] * jax.nn.sigmoid(cc[:, CC_WIDTH:])
    lo = CC_PAD - CC_HIST
    outs = []
    for q in range(ns):
        xbuf[q, CC_PAD:CC_PAD + rs, :] = x[q * rs:(q + 1) * rs]
        nb_ref[q] = xbuf[q, rs + lo:rs + CC_PAD, :]
        for a in range(1, 8):
            sh_scr[a - 1] = xbuf[q, a:a + rs + CC_PAD - 8, :]
        acc = None
        for j in range(CC_KERNEL):
            a, m = (lo + j) % 8, (lo + j) // 8
            win = sh_scr[a - 1, 8 * m:8 * m + rs, :] if a else xbuf[q, 8 * m:8 * m + rs, :]
            term = w_ref[j:j + 1, :] * win
            acc = term if acc is None else acc + term
        outs.append(acc)
    acc = (outs[0] if ns == 1 else jnp.concatenate(outs, axis=0)) + p_ref[0:1, :]
    mu = jnp.mean(acc, axis=-1, keepdims=True)
    xc = acc - mu
    var = jnp.mean(xc * xc, axis=-1, keepdims=True)
    return _silu(xc * lax.rsqrt(var + 1e-5) * p_ref[1:2, :] + p_ref[2:3, :])


FFN_SPLITS = ((0, 1024), (1024, 2048), (2048, FFN_HIDDEN))


def _out_ffn_kernel(h_ref, ya_ref, yf_ref, yc_ref, p_ref, gw_ref, gb_ref,
                    woa_ref, wob_ref, woc_ref, nf_ref, w1_ref, w3_ref, w2_ref, pew_ref, peg_ref, nfin_ref,
                    o_ref, yb_scr, ys_scr, *, tm, final):
    nck = tm // S5_CHUNK
    gpl = LANES // S5_GROUP
    for s in range(S5_CHUNK):
        for g in range(S5_GROUPS):
            ys_scr[s, g // gpl, :, (g % gpl) * S5_GROUP:(g % gpl + 1) * S5_GROUP] = (
                yf_ref[0, g, :, s * S5_GROUP:(s + 1) * S5_GROUP])
        for half in range(S5_WIDTH // LANES):
            yb_scr[half, pl.ds(s, nck, stride=S5_CHUNK), :] = ys_scr[s, half]
    yb = jnp.concatenate([yb_scr[half] for half in range(S5_WIDTH // LANES)], axis=1)
    yb = yb * jax.nn.sigmoid(_dot(yb.astype(BF16), gw_ref[...]) + gb_ref[...])
    h = h_ref[...] + (_dot(ya_ref[...].astype(BF16), woa_ref[...])
                      + _dot(yb.astype(BF16), wob_ref[...])
                      + _dot(yc_ref[...].astype(BF16), woc_ref[...]))
    ms = jnp.mean(h * h, axis=-1, keepdims=True)
    hf = ((h * lax.rsqrt(ms + 1e-6)) * nf_ref[...]).astype(BF16)
    for lo, hi in FFN_SPLITS:
        a = _dot(hf, w1_ref[:, lo:hi])
        b = _dot(hf, w3_ref[:, lo:hi])
        h = h + _dot((_silu(a) * b).astype(BF16), w2_ref[lo:hi, :])
    pe = _dot(p_ref[...].astype(BF16), pew_ref[...])
    h = h + pe * jax.nn.sigmoid(_dot(h.astype(BF16), peg_ref[...]))
    if final:
        ms = jnp.mean(h * h, axis=-1, keepdims=True)
        h = (h * lax.rsqrt(ms + 1e-6)) * nfin_ref[...]
    o_ref[...] = h


def _out_ffn(h, ya, yf, yc, p, consts, tm, tps, final):
    n = h.shape[0]
    row = lambda w: pl.BlockSpec((tm, w), lambda i: (i, 0))
    return pl.pallas_call(
        functools.partial(_out_ffn_kernel, tm=tm, final=final),
        grid=(n // tm,),
        in_specs=[row(D_MODEL), row(GDN_WIDTH), _s5_block_spec(tm, tps), row(CC_WIDTH), row(PLE_DIM)]
        + [_const_spec(a.shape) for a in consts],
        out_specs=row(D_MODEL),
        out_shape=jax.ShapeDtypeStruct((n, D_MODEL), F32),
        scratch_shapes=[pltpu.VMEM((S5_WIDTH // LANES, tm, LANES), F32),
                        pltpu.VMEM((S5_CHUNK, S5_WIDTH // LANES, tm // S5_CHUNK, LANES), F32)],
        compiler_params=_cparams(("parallel",)),
        name="out_ffn",
    )(h, ya, yf, yc, p, *consts)


def _tile_rows(n, want):
    t = min(n, want)
    while n % t:
        t //= 2
    return t


def _prep_layer(i, W):
    w_in = W['w_in'][i]
    L = {}
    L['g_mix'] = W['norm_mix'][i][None, :]
    L['wq'] = w_in[:, :OFF_Z].astype(BF16)
    L['wz'] = w_in[:, OFF_Z:OFF_BA].astype(BF16)
    L['wba'] = jnp.pad(w_in[:, OFF_BA:OFF_S5], ((0, 0), (0, LANES - 2 * GDN_HEADS))).astype(BF16)
    L['wu'] = w_in[:, OFF_S5:OFF_CC].astype(BF16)
    L['wc'] = w_in[:, OFF_CC:].astype(BF16)
    L['conv_w'] = W['gdn_conv_w'][i]
    padl = (GDN_HEADS, LANES - 2 * GDN_HEADS)
    L['ad'] = jnp.stack([jnp.pad(W['gdn_a_log'][i], padl), jnp.pad(W['gdn_dt_bias'][i], padl)])
    L['gdn_norm'] = W['gdn_norm'][i][None, :]
    lam = jnp.stack([W['s5_lam_re'][i], W['s5_lam_im'][i]])[:, :, None, :]
    bt =jnp.stack([jnp.swapaxes(W['s5_b_re'][i], 1, 2), jnp.swapaxes(W['s5_b_im'][i], 1, 2)])
    cm = jnp.stack([W['s5_c_re'][i], W['s5_c_im'][i]])
    ktoep, bend, cst, lamc = _s5_prep(lam, W['s5_log_dt'][i][:, None, None], bt, cm)
    w = S5_CHUNK * S5_GROUP
    L['ktoep'] = ktoep
    L['bend'] = bend.reshape(S5_GROUPS, w, 2 * S5_STATE)
    L['cst'] = cst.reshape(S5_GROUPS, w, 2 * S5_STATE)
    L['lamc'] = lamc.reshape(S5_GROUPS, 2 * S5_STATE)
    L['d_t'] = jnp.tile(W['s5_d'][i].reshape(S5_GROUPS, S5_GROUP), (1, S5_CHUNK))
    L['glu_w'] = W['s5_glu_w'][i].astype(BF16)
    L['glu_b'] = W['s5_glu_b'][i][None, :]
    L['dw_w'] = W['cc_dw_w'][i]
    L['cc_p'] = jnp.stack([W['cc_dw_b'][i], W['cc_ln_g'][i], W['cc_ln_b'][i]])
    wo = W['w_out'][i].astype(BF16)
    L['woa'], L['wob'], L['woc'] = wo[:GDN_WIDTH], wo[GDN_WIDTH:GDN_WIDTH + S5_WIDTH], wo[GDN_WIDTH + S5_WIDTH:]
    L['g_ffn'] = W['norm_ffn'][i][None, :]
    L['w1'] = W['ffn_w1'][i].astype(BF16)
    L['w3'] = W['ffn_w3'][i].astype(BF16)
    L['w2'] = W['ffn_w2'][i].astype(BF16)
    L['pew'] = W['pe_w'][i].astype(BF16)
    L['peg'] = W['pe_gate_w'][i].astype(BF16)
    L['g_fin'] = W['norm_final'][None, :]
    return L


def _layer(h, p, st_gdn, st_gconv, st_s5, st_conv, L, final):
    b, t, _ = h.shape
    n = b * t
    tm = _tile_rows(n, ROW_TILE)
    ns, tps = _seq_tiling(b, t, tm)
    qkv, z, ba, u_f, yc, b_conv = _in_proj(h.reshape(n, D_MODEL), L['g_mix'], st_conv, L['wq'], L['wz'], L['wba'],
                                           L['wu'], L['wc'], L['dw_w'], L['cc_p'], tm, ns, tps)

    chunk = 64 if t % 64 == 0 else t
    tt = _tile_rows(t, 512)
    ya, s_gdn, b_gconv = _gdn(qkv.reshape(b, t, QKV_COLS), z.reshape(b, t, GDN_WIDTH), ba.reshape(b, t, LANES),
                              st_gconv, st_gdn, L['conv_w'], L['ad'], L['gdn_norm'], tt, chunk)

    h0 = jnp.concatenate([st_s5[..., 0], st_s5[..., 1]], axis=-1)
    h0 = jnp.swapaxes(h0.reshape(b // ns, ns, S5_GROUPS, 2 * S5_STATE), 1, 2)
    y_f, hn = _s5(u_f, h0, L['ktoep'], L['bend'], L['cst'], L['lamc'], L['d_t'], t // S5_CHUNK)
    hn = jnp.swapaxes(hn, 1, 2).reshape(b, S5_GROUPS, 2 * S5_STATE)
    s_s5 = jnp.stack([hn[..., :S5_STATE], hn[..., S5_STATE:]], axis=-1)

    consts = [L['glu_w'], L['glu_b'], L['woa'], L['wob'], L['woc'], L['g_ffn'],
              L['w1'], L['w3'], L['w2'], L['pew'], L['peg'], L['g_fin']]
    h_new = _out_ffn(h.reshape(n, D_MODEL), ya.reshape(n, GDN_WIDTH), y_f, yc, p.reshape(n, PLE_DIM),
                     consts, tm, tps, final)
    return h_new.reshape(b, t, D_MODEL), s_gdn, b_gconv, s_s5, b_conv


def _trunk(x, p, st_gdn, st_gconv, st_s5, st_conv, layers):
    h = x
    outs = ([], [], [], [])
    depth = len(layers)
    for i, L in enumerate(layers):
        h, *st = _layer(h, p[i], st_gdn[i], st_gconv[i], st_s5[i], st_conv[i], L, i == depth - 1)
        for acc, s in zip(outs, st):
            acc.append(s)
    return (h,) + tuple(jnp.stack(o) for o in outs)


def kernel(x_prompt, x_sample, p_prompt, p_sample, state_gdn, state_gdn_conv, state_s5, state_conv, norm_mix, w_in, gdn_conv_w, gdn_a_log, gdn_dt_bias, gdn_norm, s5_lam_re, s5_lam_im, s5_log_dt, s5_b_re, s5_b_im, s5_c_re, s5_c_im, s5_d, s5_glu_w, s5_glu_b, cc_dw_w, cc_dw_b, cc_ln_g, cc_ln_b, w_out, norm_ffn, ffn_w1, ffn_w3, ffn_w2, pe_w, pe_gate_w, norm_final):
    W = dict(norm_mix=norm_mix, w_in=w_in, gdn_conv_w=gdn_conv_w, gdn_a_log=gdn_a_log, gdn_dt_bias=gdn_dt_bias,
             gdn_norm=gdn_norm, s5_lam_re=s5_lam_re, s5_lam_im=s5_lam_im, s5_log_dt=s5_log_dt, s5_b_re=s5_b_re,
             s5_b_im=s5_b_im, s5_c_re=s5_c_re, s5_c_im=s5_c_im, s5_d=s5_d, s5_glu_w=s5_glu_w, s5_glu_b=s5_glu_b,
             cc_dw_w=cc_dw_w, cc_dw_b=cc_dw_b, cc_ln_g=cc_ln_g, cc_ln_b=cc_ln_b, w_out=w_out, norm_ffn=norm_ffn,
             ffn_w1=ffn_w1, ffn_w3=ffn_w3, ffn_w2=ffn_w2, pe_w=pe_w, pe_gate_w=pe_gate_w, norm_final=norm_final)
    depth = w_in.shape[0]
    layers = [_prep_layer(i, W) for i in range(depth)]
    bp = x_prompt.shape[0]
    z_gdn = jnp.zeros((depth, bp, GDN_HEADS, GDN_HEAD_DIM, GDN_HEAD_DIM), F32)
    z_gconv = jnp.zeros((depth, bp, GDN_CONV - 1, QKV_COLS), F32)
    z_s5 = jnp.zeros((depth, bp, S5_GROUPS, S5_STATE, 2), F32)
    z_conv = jnp.zeros((depth, bp, CC_KERNEL - 1, CC_WIDTH), F32)
    y_p, gdn_p, gconv_p, s5_p, conv_p = _trunk(x_prompt, p_prompt, z_gdn, z_gconv, z_s5, z_conv, layers)
    y_s, gdn_s, gconv_s, s5_s, conv_s = _trunk(x_sample, p_sample, state_gdn, state_gdn_conv, state_s5,
                                               state_conv, layers)
    return (y_p, y_s, gdn_p, gconv_p, s5_p, conv_p, gdn_s, gconv_s, s5_s, conv_s)
```
